```python
import math
import jax, jax.numpy as jnp
from jax import lax
import numpy as np

D_MODEL = 1024
BATCH = 8
SEQ = 4096
DEPTH = 2

N_MIXERS = 2
N_MOBA = (DEPTH + 1) // 2
N_GDN = DEPTH // 2

MOBA_HEAD_DIM = 128
MOBA_HEADS = D_MODEL // MOBA_HEAD_DIM
MOBA_BLOCK = 256
MOBA_TOPK = 3
MOBA_QUERY_CHUNK = 16
ROPE_THETA = 10000.0

GDN_HEAD_DIM = 128
GDN_QK_HEADS = D_MODEL // GDN_HEAD_DIM
GDN_V_HEADS = 2 * GDN_QK_HEADS
GDN_QK_DIM = GDN_QK_HEADS * GDN_HEAD_DIM
GDN_V_DIM = GDN_V_HEADS * GDN_HEAD_DIM
GDN_CONV_DIM = 2 * GDN_QK_DIM + GDN_V_DIM
GDN_PROJ_DIM = GDN_CONV_DIM + GDN_V_DIM + 2 * GDN_V_HEADS
GDN_CONV = 4
GDN_CHUNK = 64

D_FF = 2816
HALF_STEP = 0.5

DEEPNORM_ALPHA = (2 * DEPTH) ** 0.25
DEEPNORM_BETA = (8 * DEPTH) ** -0.25
LN_EPS = 1e-5
RMS_EPS = 1e-6
NEG_INF = -1e30

kernel_name = "hybrid_moba_gdn_macaron_deepnorm"

f32 = jnp.float32


def layer_norm(x, g, b):
    xf = x.astype(f32)
    mu = xf.mean(-1, keepdims=True)
    var = jnp.square(xf - mu).mean(-1, keepdims=True)
    y = (xf - mu) * lax.rsqrt(var + LN_EPS) * g.astype(f32) + b.astype(f32)
    return y.astype(x.dtype)


def swiglu_ffn(x, w_in, w_out):
    gate, up = jnp.split(x @ w_in, 2, axis=-1)
    return (jax.nn.silu(gate) * up) @ w_out


def rope(x, pos):
    half = x.shape[-1] // 2
    inv_freq = ROPE_THETA ** (-jnp.arange(half, dtype=f32) / half)
    ang = pos.astype(f32)[:, None] * inv_freq[None, :]
    cos = jnp.cos(ang)[None, :, None, :]
    sin = jnp.sin(ang)[None, :, None, :]
    x1 = x[..., :half].astype(f32)
    x2 = x[..., half:].astype(f32)
    out = jnp.concatenate([x1 * cos - x2 * sin, x2 * cos + x1 * sin], axis=-1)
    return out.astype(x.dtype)


def moba_attention(x, w_in, w_out):
    B, T, _ = x.shape
    H, Dh, BS, QC = MOBA_HEADS, MOBA_HEAD_DIM, MOBA_BLOCK, MOBA_QUERY_CHUNK
    q, k, v = jnp.split(x @ w_in, 3, axis=-1)
    pos = jnp.arange(T)
    q = rope(q.reshape(B, T, H, Dh), pos)
    k = rope(k.reshape(B, T, H, Dh), pos)
    v = v.reshape(B, T, H, Dh)
    n_blk = -(-T // BS)
    Tp = n_blk * BS
    pad = ((0, 0), (0, 0), (0, Tp - T), (0, 0))
    q, k, v = [jnp.pad(a.transpose(0, 2, 1, 3), pad) for a in (q, k, v)]
    kb = k.reshape(B, H, n_blk, BS, Dh)
    vb = v.reshape(B, H, n_blk, BS, Dh)
    scale = Dh ** -0.5

    k_mean = kb.astype(f32).mean(axis=3)
    gate = jnp.einsum('bhtd,bhnd->bhtn', q.astype(f32), k_mean)
    q_blk = jnp.arange(Tp) // BS
    past = jnp.arange(n_blk)[None, :] < q_blk[:, None]
    gate = jnp.where(past, gate, NEG_INF)
    n_sel = min(MOBA_TOPK, n_blk)
    _, sel = lax.top_k(gate, n_sel)
    sel_valid = sel < q_blk[:, None]

    n_qc = Tp // QC
    q_c = q.reshape(B, H, n_qc, QC, Dh).transpose(2, 0, 1, 3, 4)
    sel_c = sel.reshape(B, H, n_qc, QC, n_sel).transpose(2, 0, 1, 3, 4)
    valid_c = sel_valid.reshape(B, H, n_qc, QC, n_sel).transpose(2, 0, 1, 3, 4)
    b_idx = jnp.arange(B)[:, None, None, None]
    h_idx = jnp.arange(H)[None, :, None, None]

    def attend_chunk(args):
        c, qq, ss, vv = args
        t0 = c * QC
        blk = t0 // BS
        k_sel = kb[b_idx, h_idx, ss]
        v_sel = vb[b_idx, h_idx, ss]
        k_own = lax.dynamic_index_in_dim(kb, blk, axis=2, keepdims=False)
        v_own = lax.dynamic_index_in_dim(vb, blk, axis=2, keepdims=False)
        s_past = jnp.einsum('bhqd,bhqspd->bhqsp', qq, k_sel).astype(f32) * scale
        s_past = jnp.where(vv[..., None], s_past, NEG_INF).reshape(B, H, QC, n_sel * BS)
        s_own = jnp.einsum('bhqd,bhpd->bhqp', qq, k_own).astype(f32) * scale
        q_off = t0 - blk * BS + jnp.arange(QC)
        causal = jnp.arange(BS)[None, :] <= q_off[:, None]
        s_own = jnp.where(causal, s_own, NEG_INF)
        p = jax.nn.softmax(jnp.concatenate([s_past, s_own], axis=-1), axis=-1).astype(v.dtype)
        p_past = p[..., :n_sel * BS].reshape(B, H, QC, n_sel, BS)
        p_own = p[..., n_sel * BS:]
        return (jnp.einsum('bhqsp,bhqspd->bhqd', p_past, v_sel)
                + jnp.einsum('bhqp,bhpd->bhqd', p_own, v_own))

    o = lax.map(attend_chunk, (jnp.arange(n_qc), q_c, sel_c, valid_c))
    o = o.transpose(1, 0, 3, 2, 4).reshape(B, Tp, H * Dh)[:, :T]
    return o @ w_out


def causal_depthwise_conv(x, w):
    K, C = w.shape
    return lax.conv_general_dilated(
        x, w[:, None, :].astype(x.dtype), window_strides=(1,), padding=[(K - 1, 0)],
        dimension_numbers=('NWC', 'WIO', 'NWC'), feature_group_count=C)


def l2norm(x):
    xf = x.astype(f32)
    return xf * lax.rsqrt(jnp.sum(xf * xf, axis=-1, keepdims=True) + RMS_EPS)


def chunk_gated_delta_rule(q, k, v, g, beta):
    B, T, H, Dk = q.shape
    Dv = v.shape[-1]
    C = GDN_CHUNK
    n = T // C

    def chunked(a):
        a = a.astype(f32).reshape((B, n, C, H) + a.shape[3:])
        return jnp.moveaxis(a, 3, 1)

    q, k, v, g, beta = [chunked(a) for a in (q, k, v, g, beta)]
    gc = jnp.cumsum(g, axis=-1)
    lower_incl = jnp.tril(jnp.ones((C, C), bool))
    strict = jnp.tril(jnp.ones((C, C), bool), -1)
    decay = jnp.exp(jnp.where(lower_incl, gc[..., :, None] - gc[..., None, :], NEG_INF))
    kb = k * beta[..., None]
    a_mat = jnp.where(strict, jnp.einsum('bhnid,bhnjd->bhnij', kb, k) * decay, 0.0)
    rhs = jnp.concatenate([v * beta[..., None], kb * jnp.exp(gc)[..., None]], axis=-1)
    uw = lax.linalg.triangular_solve(a_mat, rhs, left_side=True, lower=True, unit_diagonal=True)
    u, w = uw[..., :Dv], uw[..., Dv:]
    attn_intra = jnp.where(lower_incl, jnp.einsum('bhnid,bhnjd->bhnij', q, k) * decay, 0.0)
    q_dec = q * jnp.exp(gc)[..., None]
    k_dec = k * jnp.exp(gc[..., -1:] - gc)[..., None]
    g_last = jnp.exp(gc[..., -1])
    xs = tuple(jnp.moveaxis(a, 2, 0) for a in (u, w, attn_intra, q_dec, k_dec, g_last))

    def step(S, inp):
        u_c, w_c, a_c, q_c, k_c, gl = inp
        v_new = u_c - jnp.einsum('bhck,bhkv->bhcv', w_c, S)
        o_c = jnp.einsum('bhck,bhkv->bhcv', q_c, S) + jnp.einsum('bhij,bhjv->bhiv', a_c, v_new)
        S = S * gl[..., None, None] + jnp.einsum('bhck,bhcv->bhkv', k_c, v_new)
        return S, o_c

    S0 = jnp.zeros((B, H, Dk, Dv), f32)
    _, o = lax.scan(step, S0, xs)
    return o.transpose(1, 0, 3, 2, 4).reshape(B, T, H, Dv)


def gated_deltanet(x, w_in, conv_w, a_log, dt_bias, norm_w, w_out):
    B, T, _ = x.shape
    Hk, Hv, Dh = GDN_QK_HEADS, GDN_V_HEADS, GDN_HEAD_DIM
    proj = x @ w_in
    qkv, z, b, a = jnp.split(
        proj, [GDN_CONV_DIM, GDN_CONV_DIM + GDN_V_DIM, GDN_CONV_DIM + GDN_V_DIM + Hv], axis=-1)
    qkv = jax.nn.silu(causal_depthwise_conv(qkv, conv_w))
    q, k, v = jnp.split(qkv, [GDN_QK_DIM, 2 * GDN_QK_DIM], axis=-1)
    rep = Hv // Hk
    q = jnp.repeat(q.reshape(B, T, Hk, Dh), rep, axis=2)
    k = jnp.repeat(k.reshape(B, T, Hk, Dh), rep, axis=2)
    v = v.reshape(B, T, Hv, Dh)
    q = l2norm(q) * (Dh ** -0.5)
    k = l2norm(k)
    beta = jax.nn.sigmoid(b.astype(f32))
    g = -jnp.exp(a_log.astype(f32)) * jax.nn.softplus(a.astype(f32) + dt_bias.astype(f32))
    o = chunk_gated_delta_rule(q, k, v, g, beta)
    zf = z.reshape(B, T, Hv, Dh).astype(f32)
    o = o * lax.rsqrt(jnp.mean(o * o, axis=-1, keepdims=True) + RMS_EPS) * norm_w.astype(f32) * jax.nn.silu(zf)
    return o.reshape(B, T, GDN_V_DIM).astype(x.dtype) @ w_out


def setup_inputs(seed: int = 0) -> dict:
    key = jax.random.key(seed)
    ks = jax.random.split(key, 16)
    D = D_MODEL

    def normal(k, shape, fan_in, scale=1.0):
        return jax.random.normal(k, shape, f32) * (scale * fan_in ** -0.5)

    x = jax.random.normal(ks[0], (BATCH, SEQ, D), f32)
    ln_g = 1.0 + 0.02 * jax.random.normal(ks[1], (DEPTH, 3, D), f32)
    ln_b = 0.02 * jax.random.normal(ks[2], (DEPTH, 3, D), f32)
    ffn_pre_w_in = normal(ks[3], (DEPTH, D, 2 * D_FF), D)
    ffn_pre_w_out = normal(ks[4], (DEPTH, D_FF, D), D_FF, DEEPNORM_BETA)
    ffn_post_w_in = normal(ks[5], (DEPTH, D, 2 * D_FF), D)
    ffn_post_w_out = normal(ks[6], (DEPTH, D_FF, D), D_FF, DEEPNORM_BETA)
    moba_dim = MOBA_HEADS * MOBA_HEAD_DIM
    moba_w_in = normal(ks[7], (N_MOBA, D, 3 * moba_dim), D)
    moba_w_out = normal(ks[8], (N_MOBA, moba_dim, D), moba_dim, DEEPNORM_BETA)
    gdn_w_in = normal(ks[9], (N_GDN, D, GDN_PROJ_DIM), D)
    gdn_conv_w = normal(ks[10], (N_GDN, GDN_CONV, GDN_CONV_DIM), GDN_CONV)
    gdn_a_log = jnp.log(jax.random.uniform(ks[11], (N_GDN, GDN_V_HEADS), f32, 1.0, 16.0))
    dt = jnp.exp(jax.random.uniform(ks[12], (N_GDN, GDN_V_HEADS), f32, math.log(1e-3), math.log(1e-1)))
    gdn_dt_bias = dt + jnp.log(-jnp.expm1(-dt))
    gdn_norm_w = 1.0 + 0.02 * jax.random.normal(ks[13], (N_GDN, GDN_HEAD_DIM), f32)
    gdn_w_out = normal(ks[14], (N_GDN, GDN_V_DIM, D), GDN_V_DIM, DEEPNORM_BETA)
    return {"x": x, "ln_g": ln_g, "ln_b": ln_b,
            "ffn_pre_w_in": ffn_pre_w_in, "ffn_pre_w_out": ffn_pre_w_out,
            "ffn_post_w_in": ffn_post_w_in, "ffn_post_w_out": ffn_post_w_out,
            "moba_w_in": moba_w_in, "moba_w_out": moba_w_out,
            "gdn_w_in": gdn_w_in, "gdn_conv_w": gdn_conv_w, "gdn_a_log": gdn_a_log,
            "gdn_dt_bias": gdn_dt_bias, "gdn_norm_w": gdn_norm_w, "gdn_w_out": gdn_w_out}


def reference(x, ln_g, ln_b, ffn_pre_w_in, ffn_pre_w_out, ffn_post_w_in, ffn_post_w_out,
              moba_w_in, moba_w_out, gdn_w_in, gdn_conv_w, gdn_a_log, gdn_dt_bias,
              gdn_norm_w, gdn_w_out):
    h = x
    for i in range(DEPTH):
        h = layer_norm(DEEPNORM_ALPHA * h + HALF_STEP * swiglu_ffn(h, ffn_pre_w_in[i], ffn_pre_w_out[i]),
                       ln_g[i, 0], ln_b[i, 0])
        j = i // N_MIXERS
        if i % N_MIXERS == 0:
            mix = moba_attention(h, moba_w_in[j], moba_w_out[j])
        else:
            mix = gated_deltanet(h, gdn_w_in[j], gdn_conv_w[j], gdn_a_log[j], gdn_dt_bias[j],
                                 gdn_norm_w[j], gdn_w_out[j])
        h = layer_norm(DEEPNORM_ALPHA * h + mix, ln_g[i, 1], ln_b[i, 1])
        h = layer_norm(DEEPNORM_ALPHA * h + HALF_STEP * swiglu_ffn(h, ffn_post_w_in[i], ffn_post_w_out[i]),
                       ln_g[i, 2], ln_b[i, 2])
    return h
```

```python
import functools
import math

import jax
import jax.numpy as jnp
from jax import lax
from jax.experimental import pallas as pl
from jax.experimental.pallas import tpu as pltpu

F32 = jnp.float32
BF16 = jnp.bfloat16

D_MODEL = 1024
DEPTH = 2
D_FF = 2816
HEAD_DIM = 128
MOBA_HEADS = D_MODEL // HEAD_DIM
MOBA_BLOCK = 256
MOBA_TOPK = 3
ROPE_THETA = 10000.0
GDN_QK_HEADS = D_MODEL // HEAD_DIM
GDN_V_HEADS = 2 * GDN_QK_HEADS
GDN_QK_DIM = GDN_QK_HEADS * HEAD_DIM
GDN_V_DIM = GDN_V_HEADS * HEAD_DIM
GDN_CONV_DIM = 2 * GDN_QK_DIM + GDN_V_DIM
GDN_CONV = 4
GDN_CHUNK = 64
ALPHA = (2 * DEPTH) ** 0.25
HALF_STEP = 0.5
LN_EPS = 1e-5
RMS_EPS = 1e-6
NEG_INF = -1e30

LANES = 128
SUBLANES = 8
VMEM_LIMIT = 56 * 1024 * 1024

NT_DIMS = (((1,), (1,)), ((), ()))
TN_DIMS = (((0,), (0,)), ((), ()))


def _dot(a, b):
    return jnp.dot(a, b, preferred_element_type=F32)


def _dot_nt(a, b):
    return lax.dot_general(a, b, NT_DIMS, preferred_element_type=F32)


def _sigmoid(x):
    return 1.0 / (1.0 + jnp.exp(-x))


def _layer_norm(y, g, b):
    mu = jnp.mean(y, axis=-1, keepdims=True)
    d = y - mu
    var = jnp.mean(d * d, axis=-1, keepdims=True)
    return d * lax.rsqrt(var + LN_EPS) * g + b


def _resident(shape):
    nd = len(shape)
    return pl.BlockSpec(shape, lambda *_: (0,) * nd, pipeline_mode=pl.Buffered(1))


def _params(semantics):
    return pltpu.CompilerParams(dimension_semantics=semantics, vmem_limit_bytes=VMEM_LIMIT)


FFN_ROWS = 512
FFN_CHUNK = 1408


def _ffn_kernel(h_ref, win_ref, wout_ref, g_ref, b_ref, o_ref):
    h = h_ref[...]
    hb = h.astype(BF16)
    acc = None
    for j in range(D_FF // FFN_CHUNK):
        lo = j * FFN_CHUNK
        gate = _dot(hb, win_ref[:, lo:lo + FFN_CHUNK])
        up = _dot(hb, win_ref[:, D_FF + lo:D_FF + lo + FFN_CHUNK])
        act = (gate * _sigmoid(gate) * up).astype(BF16)
        part = _dot(act, wout_ref[lo:lo + FFN_CHUNK, :])
        acc = part if acc is None else acc + part
    y = ALPHA * h + HALF_STEP * acc
    o_ref[...] = _layer_norm(y, g_ref[...], b_ref[...])


def _ffn_block(h, w_in, w_out, g, b):
    m, d = h.shape
    return pl.pallas_call(
        _ffn_kernel,
        grid=(m // FFN_ROWS,),
        in_specs=[
            pl.BlockSpec((FFN_ROWS, d), lambda i: (i, 0)),
            _resident(w_in.shape),
            _resident(w_out.shape),
            _resident(g.shape),
            _resident(b.shape),
        ],
        out_specs=pl.BlockSpec((FFN_ROWS, d), lambda i: (i, 0)),
        out_shape=jax.ShapeDtypeStruct((m, d), F32),
        compiler_params=_params(("parallel",)),
        name="ffn_ln",
    )(h, w_in, w_out, g, b)


PROJ_ROWS = 512


def _proj_ln_kernel(h_ref, x_ref, w_ref, g_ref, b_ref, o_ref):
    y = ALPHA * h_ref[...] + _dot(x_ref[...], w_ref[...])
    o_ref[...] = _layer_norm(y, g_ref[...], b_ref[...])


def _proj_ln(h, x, w, g, b):
    m, d = h.shape
    return pl.pallas_call(
        _proj_ln_kernel,
        grid=(m // PROJ_ROWS,),
        in_specs=[
            pl.BlockSpec((PROJ_ROWS, d), lambda i: (i, 0)),
            pl.BlockSpec((PROJ_ROWS, x.shape[1]), lambda i: (i, 0)),
            _resident(w.shape),
            _resident(g.shape),
            _resident(b.shape),
        ],
        out_specs=pl.BlockSpec((PROJ_ROWS, d), lambda i: (i, 0)),
        out_shape=jax.ShapeDtypeStruct((m, d), F32),
        compiler_params=_params(("parallel",)),
        name="proj_ln",
    )(h, x, w, g, b)


QKV_ROWS = 512


def _qkv_rope_kernel(h_ref, w_ref, cos_ref, sin_ref, q_ref, k_ref, v_ref):
    hb = h_ref[...].astype(BF16)
    cos = cos_ref[...]
    sin = sin_ref[...]
    for part, out_ref in ((0, q_ref), (1, k_ref)):
        x = _dot(hb, w_ref[:, part * D_MODEL:(part + 1) * D_MODEL])
        for hh in range(MOBA_HEADS):
            xs = x[:, hh * HEAD_DIM:(hh + 1) * HEAD_DIM]
            rot = pltpu.roll(xs, HEAD_DIM // 2, axis=1)
            out_ref[:, hh * HEAD_DIM:(hh + 1) * HEAD_DIM] = (xs * cos + rot * sin).astype(BF16)
    v_ref[...] = _dot(hb, w_ref[:, 2 * D_MODEL:3 * D_MODEL]).astype(BF16)


def _rope_tables(seq):
    half = HEAD_DIM // 2
    inv_freq = ROPE_THETA ** (-jnp.arange(half, dtype=F32) / half)
    ang = jnp.arange(seq).astype(F32)[:, None] * inv_freq[None, :]
    cos = jnp.cos(ang)
    sin = jnp.sin(ang)
    return jnp.concatenate([cos, cos], axis=-1), jnp.concatenate([-sin, sin], axis=-1)


def _qkv_rope(h, w, cos, sin, seq):
    m, d = h.shape
    t_tiles = seq // QKV_ROWS
    out = jax.ShapeDtypeStruct((m, d), BF16)
    row_spec = pl.BlockSpec((QKV_ROWS, d), lambda i: (i, 0))
    tab_spec = pl.BlockSpec((QKV_ROWS, HEAD_DIM), lambda i: (i % t_tiles, 0))
    return pl.pallas_call(
        _qkv_rope_kernel,
        grid=(m // QKV_ROWS,),
        in_specs=[row_spec, _resident(w.shape), tab_spec, tab_spec],
        out_specs=[row_spec, row_spec, row_spec],
        out_shape=[out, out, out],
        compiler_params=_params(("parallel",)),
        name="moba_qkv_rope",
    )(h, w, cos, sin)


def _moba_attn_kernel(q_ref, k_ref, v_ref, o_ref, kmean_ref, m_ref, l_ref, acc_ref, *, n_blk):
    qi = pl.program_id(2)
    bs = MOBA_BLOCK
    scale = HEAD_DIM ** -0.5

    @pl.when(qi == 0)
    def _():
        kmean_ref[...] = jnp.zeros_like(kmean_ref)
        for n in range(n_blk):
            kb = k_ref[n * bs:(n + 1) * bs, :].astype(F32)
            kmean_ref[n:n + 1, :] = jnp.sum(kb, axis=0, keepdims=True) * (1.0 / bs)

    q = q_ref[...]

    gate = _dot_nt(q, kmean_ref[...].astype(BF16))
    lane = lax.broadcasted_iota(jnp.int32, gate.shape, 1)
    lane_f = lane.astype(F32)
    past = lane < qi
    gate = jnp.where(past, gate, NEG_INF)
    sel = jnp.zeros(gate.shape, F32)
    for _ in range(min(MOBA_TOPK, n_blk)):
        top = jnp.max(gate, axis=-1, keepdims=True)
        first = jnp.min(jnp.where(gate == top, lane_f, float(LANES)), axis=-1, keepdims=True)
        pick = lane_f == first
        sel = jnp.where(jnp.logical_and(pick, past), 1.0, sel)
        gate = jnp.where(pick, -jnp.inf, gate)

    own = pl.multiple_of(qi * bs, bs)
    s = _dot_nt(q, k_ref[pl.ds(own, bs), :]) * scale
    row = lax.broadcasted_iota(jnp.int32, s.shape, 0)
    col = lax.broadcasted_iota(jnp.int32, s.shape, 1)
    s = jnp.where(col <= row, s, NEG_INF)
    m0 = jnp.max(s, axis=-1, keepdims=True)
    p = jnp.exp(s - m0)
    m_ref[...] = m0
    l_ref[...] = jnp.sum(p, axis=-1, keepdims=True)
    acc_ref[...] = _dot(p.astype(BF16), v_ref[pl.ds(own, bs), :])

    def past_block(n, carry):
        off = pl.multiple_of(n * bs, bs)
        s = _dot_nt(q, k_ref[pl.ds(off, bs), :]) * scale
        chosen = jnp.max(jnp.where(lane == n, sel, 0.0), axis=-1, keepdims=True)
        s = jnp.where(chosen > 0.5, s, NEG_INF)
        m_old = m_ref[...]
        m_new = jnp.maximum(m_old, jnp.max(s, axis=-1, keepdims=True))
        corr = jnp.exp(m_old - m_new)
        p = jnp.exp(s - m_new)
        l_ref[...] = corr * l_ref[...] + jnp.sum(p, axis=-1, keepdims=True)
        acc_ref[...] = corr * acc_ref[...] + _dot(p.astype(BF16), v_ref[pl.ds(off, bs), :])
        m_ref[...] = m_new
        return carry

    lax.fori_loop(0, qi, past_block, 0)
    o_ref[...] = (acc_ref[...] / l_ref[...]).astype(o_ref.dtype)


def _moba_attention(q, k, v, batch, seq):
    n_blk = seq // MOBA_BLOCK
    assert seq % MOBA_BLOCK == 0 and n_blk <= LANES
    q_spec = pl.BlockSpec((None, MOBA_BLOCK, HEAD_DIM), lambda b, h, i: (b, i, h))
    kv_spec = pl.BlockSpec((None, seq, HEAD_DIM), lambda b, h, i: (b, 0, h))
    return pl.pallas_call(
        functools.partial(_moba_attn_kernel, n_blk=n_blk),
        grid=(batch, MOBA_HEADS, n_blk),
        in_specs=[q_spec, kv_spec, kv_spec],
        out_specs=q_spec,
        out_shape=jax.ShapeDtypeStruct(q.shape, BF16),
        scratch_shapes=[
            pltpu.VMEM((LANES, HEAD_DIM), F32),
            pltpu.VMEM((MOBA_BLOCK, 1), F32),
            pltpu.VMEM((MOBA_BLOCK, 1), F32),
            pltpu.VMEM((MOBA_BLOCK, HEAD_DIM), F32),
        ],
        compiler_params=_params(("parallel", "parallel", "arbitrary")),
        name="moba_attn",
    )(q, k, v)


GDN_IN_ROWS = 256


def _gdn_in_kernel(h_ref, wqkv_ref, wb_ref, wa_ref, cw_ref, alog_ref, dtb_ref,
                   q_ref, k_ref, v_ref, beta_ref, gc_ref, xs_ref):
    tm = GDN_IN_ROWS
    halo = SUBLANES
    hb = h_ref[...].astype(BF16)

    @pl.when(pl.program_id(1) == 0)
    def _():
        xs_ref[0:halo, :] = jnp.zeros((halo, GDN_CONV_DIM), F32)

    xs_ref[halo:halo + tm, :] = _dot(hb, wqkv_ref[...])

    n_cols = GDN_CONV_DIM // HEAD_DIM
    for c in range(n_cols):
        cs = slice(c * HEAD_DIM, (c + 1) * HEAD_DIM)
        y = None
        for i in range(GDN_CONV):
            r0 = halo - GDN_CONV + 1 + i
            term = xs_ref[r0:r0 + tm, cs] * cw_ref[i:i + 1, cs]
            y = term if y is None else y + term
        y = y * _sigmoid(y)
        if c < 2 * GDN_QK_HEADS:
            y = y * lax.rsqrt(jnp.sum(y * y, axis=-1, keepdims=True) + RMS_EPS)
            if c < GDN_QK_HEADS:
                q_ref[:, cs] = (y * (HEAD_DIM ** -0.5)).astype(BF16)
            else:
                c2 = c - GDN_QK_HEADS
                k_ref[:, c2 * HEAD_DIM:(c2 + 1) * HEAD_DIM] = y.astype(BF16)
        else:
            c2 = c - 2 * GDN_QK_HEADS
            v_ref[:, c2 * HEAD_DIM:(c2 + 1) * HEAD_DIM] = y.astype(BF16)

    xs_ref[0:halo, :] = xs_ref[tm:tm + halo, :]

    beta = _sigmoid(_dot(hb, wb_ref[...]))
    a = _dot(hb, wa_ref[...]) + dtb_ref[...]
    softplus = jnp.maximum(a, 0.0) + jnp.log1p(jnp.exp(-jnp.abs(a)))
    g = -jnp.exp(alog_ref[...]) * softplus
    ri = lax.broadcasted_iota(jnp.int32, (tm, tm), 0)
    ci = lax.broadcasted_iota(jnp.int32, (tm, tm), 1)
    tri = jnp.logical_and(ri >= ci, ri // GDN_CHUNK == ci // GDN_CHUNK).astype(F32)
    gc = jnp.dot(tri, g, preferred_element_type=F32, precision=lax.Precision.HIGHEST)
    beta_ref[...] = beta[:, :GDN_V_HEADS]
    gc_ref[...] = gc[:, :GDN_V_HEADS]


def _gdn_in(h3, wqkv, wb, wa, conv_w, alog, dtb):
    batch, seq, d = h3.shape
    tm = GDN_IN_ROWS

    def rows(width):
        return pl.BlockSpec((None, tm, width), lambda b, t: (b, t, 0))

    def out(width, dtype):
        return jax.ShapeDtypeStruct((batch, seq, width), dtype)

    return pl.pallas_call(
        _gdn_in_kernel,
        grid=(batch, seq // tm),
        in_specs=[rows(d), _resident(wqkv.shape), _resident(wb.shape), _resident(wa.shape),
                  _resident(conv_w.shape), _resident(alog.shape), _resident(dtb.shape)],
        out_specs=[rows(GDN_QK_DIM), rows(GDN_QK_DIM), rows(GDN_V_DIM),
                   rows(GDN_V_HEADS), rows(GDN_V_HEADS)],
        out_shape=[out(GDN_QK_DIM, BF16), out(GDN_QK_DIM, BF16), out(GDN_V_DIM, BF16),
                   out(GDN_V_HEADS, F32), out(GDN_V_HEADS, F32)],
        scratch_shapes=[pltpu.VMEM((tm + SUBLANES, GDN_CONV_DIM), F32)],
        compiler_params=_params(("parallel", "arbitrary")),
        name="gdn_in",
    )(h3, wqkv, wb, wa, conv_w, alog, dtb)


GDN_SCAN_ROWS = 256


def _inv_unit_lower(a, eye):
    n = a.shape[0]
    t = eye - a
    p = a.astype(BF16)
    k = 2
    while k < n:
        p32 = _dot(p, p)
        p = p32.astype(BF16)
        t = t + _dot(t.astype(BF16), p)
        k *= 2
    return t


def _gdn_scan_kernel(q_ref, k_ref, v_ref, gcol_ref, bcol_ref, grow_ref, o_ref, s_ref):
    hk = pl.program_id(1)
    c_len = GDN_CHUNK
    n_chunks = GDN_SCAN_ROWS // c_len
    group = GDN_V_HEADS // GDN_QK_HEADS

    @pl.when(pl.program_id(2) == 0)
    def _():
        s_ref[...] = jnp.zeros_like(s_ref)

    ii = lax.broadcasted_iota(jnp.int32, (c_len, c_len), 0)
    jj = lax.broadcasted_iota(jnp.int32, (c_len, c_len), 1)
    lower = ii >= jj
    strict = ii > jj
    eye = (ii == jj).astype(F32)
    head_lane = lax.broadcasted_iota(jnp.int32, (c_len, GDN_V_HEADS), 1)

    for c in range(n_chunks):
        rows = slice(c * c_len, (c + 1) * c_len)
        q_c = q_ref[rows, :]
        k_c = k_ref[rows, :]
        q32 = q_c.astype(F32)
        k32 = k_c.astype(F32)
        kk = _dot_nt(k_c, k_c)
        qk = _dot_nt(q_c, k_c)
        for j in range(group):
            hv = hk * group + j
            pick = head_lane == hv
            gcc = jnp.sum(jnp.where(pick, gcol_ref[rows, :], 0.0), axis=-1, keepdims=True)
            beta = jnp.sum(jnp.where(pick, bcol_ref[rows, :], 0.0), axis=-1, keepdims=True)
            gcr = grow_ref[j, c:c + 1, :]
            decay = jnp.exp(jnp.where(lower, gcc - gcr, NEG_INF))
            a_mat = jnp.where(strict, beta * kk * decay, 0.0)
            t_mat = _inv_unit_lower(a_mat, eye)
            gam = jnp.exp(gcc)
            v32 = v_ref[rows, j * HEAD_DIM:(j + 1) * HEAD_DIM].astype(F32)
            rhs = jnp.concatenate([v32 * beta, k32 * (beta * gam)], axis=1).astype(BF16)
            uw = _dot(t_mat.astype(BF16), rhs)
            u = uw[:, :HEAD_DIM]
            w = uw[:, HEAD_DIM:]
            attn = jnp.where(lower, qk * decay, 0.0).astype(BF16)
            g_last = gcc[c_len - 1:c_len, :]
            q_dec = (q32 * gam).astype(BF16)
            k_dec = (k32 * jnp.exp(g_last - gcc)).astype(BF16)
            s_old = s_ref[j]
            s_b = s_old.astype(BF16)
            v_new = u - _dot(w.astype(BF16), s_b)
            v_new_b = v_new.astype(BF16)
            o_ref[rows, j * HEAD_DIM:(j + 1) * HEAD_DIM] = _dot(q_dec, s_b) + _dot(attn, v_new_b)
            s_ref[j] = s_old * jnp.exp(g_last) + lax.dot_general(
                k_dec, v_new_b, TN_DIMS, preferred_element_type=F32)


def _gdn_scan(q, k, v, gc, beta):
    batch, seq, _ = q.shape
    tc = GDN_SCAN_ROWS
    n_chunks = tc // GDN_CHUNK
    group = GDN_V_HEADS // GDN_QK_HEADS
    g_row = gc.reshape(batch, seq // tc, n_chunks, GDN_CHUNK, GDN_V_HEADS).transpose(0, 4, 1, 2, 3)
    qk_spec = pl.BlockSpec((None, tc, HEAD_DIM), lambda b, h, t: (b, t, h))
    v_spec = pl.BlockSpec((None, tc, group * HEAD_DIM), lambda b, h, t: (b, t, h))
    col_spec = pl.BlockSpec((None, tc, GDN_V_HEADS), lambda b, h, t: (b, t, 0))
    row_spec = pl.BlockSpec((None, group, None, n_chunks, GDN_CHUNK), lambda b, h, t: (b, h, t, 0, 0))
    return pl.pallas_call(
        _gdn_scan_kernel,
        grid=(batch, GDN_QK_HEADS, seq // tc),
        in_specs=[qk_spec, qk_spec, v_spec, col_spec, col_spec, row_spec],
        out_specs=v_spec,
        out_shape=jax.ShapeDtypeStruct((batch, seq, GDN_V_DIM), F32),
        scratch_shapes=[pltpu.VMEM((group, HEAD_DIM, HEAD_DIM), F32)],
        compiler_params=_params(("parallel", "parallel", "arbitrary")),
        name="gdn_scan",
    )(q, k, v, gc, beta, g_row)


GDN_OUT_ROWS = 256


def _gdn_out_kernel(h_ref, o_ref, wz_ref, nw_ref, wo_ref, g_ref, b_ref, out_ref, y_ref):
    h = h_ref[...]
    z = _dot(h.astype(BF16), wz_ref[...])
    nw = nw_ref[...]
    for n in range(GDN_V_HEADS):
        cs = slice(n * HEAD_DIM, (n + 1) * HEAD_DIM)
        o = o_ref[:, cs]
        zz = z[:, cs]
        y = o * lax.rsqrt(jnp.mean(o * o, axis=-1, keepdims=True) + RMS_EPS) * nw * (zz * _sigmoid(zz))
        y_ref[:, cs] = y.astype(BF16)
    y = ALPHA * h + _dot(y_ref[...], wo_ref[...])
    out_ref[...] = _layer_norm(y, g_ref[...], b_ref[...])


def _gdn_out(h, o, wz, nw, wo, g, b):
    m, d = h.shape
    tm = GDN_OUT_ROWS
    return pl.pallas_call(
        _gdn_out_kernel,
        grid=(m // tm,),
        in_specs=[pl.BlockSpec((tm, d), lambda i: (i, 0)),
                  pl.BlockSpec((tm, GDN_V_DIM), lambda i: (i, 0)),
                  _resident(wz.shape), _resident(nw.shape), _resident(wo.shape),
                  _resident(g.shape), _resident(b.shape)],
        out_specs=pl.BlockSpec((tm, d), lambda i: (i, 0)),
        out_shape=jax.ShapeDtypeStruct((m, d), F32),
        scratch_shapes=[pltpu.VMEM((tm, GDN_V_DIM), BF16)],
        compiler_params=_params(("parallel",)),
        name="gdn_out",
    )(h, o, wz, nw, wo, g, b)


def _row(vec):
    return vec.reshape(1, -1).astype(F32)


def _pad_lanes(x2d):
    return jnp.pad(x2d, ((0, 0), (0, LANES - x2d.shape[1])))


def kernel(x, ln_g, ln_b, ffn_pre_w_in, ffn_pre_w_out, ffn_post_w_in, ffn_post_w_out,
           moba_w_in, moba_w_out, gdn_w_in, gdn_conv_w, gdn_a_log, gdn_dt_bias,
           gdn_norm_w, gdn_w_out):
    batch, seq, d = x.shape
    m = batch * seq
    h = x.reshape(m, d).astype(F32)
    cos, sin = _rope_tables(seq)

    for i in range(DEPTH):
        h = _ffn_block(h, ffn_pre_w_in[i].astype(BF16), ffn_pre_w_out[i].astype(BF16),
                       _row(ln_g[i, 0]), _row(ln_b[i, 0]))
        j = i // 2
        if i % 2 == 0:
            q, k, v = _qkv_rope(h, moba_w_in[j].astype(BF16), cos, sin, seq)
            shape3 = (batch, seq, d)
            o = _moba_attention(q.reshape(shape3), k.reshape(shape3), v.reshape(shape3), batch, seq)
            h = _proj_ln(h, o.reshape(m, d), moba_w_out[j].astype(BF16),
                         _row(ln_g[i, 1]), _row(ln_b[i, 1]))
        else:
            w_in = gdn_w_in[j]
            z0 = GDN_CONV_DIM
            b0 = z0 + GDN_V_DIM
            a0 = b0 + GDN_V_HEADS
            wqkv = w_in[:, :z0].astype(BF16)
            wz = w_in[:, z0:b0].astype(BF16)
            wb = _pad_lanes(w_in[:, b0:a0]).astype(BF16)
            wa = _pad_lanes(w_in[:, a0:a0 + GDN_V_HEADS]).astype(BF16)
            q, k, v, beta, gc = _gdn_in(
                h.reshape(batch, seq, d), wqkv, wb, wa, gdn_conv_w[j].astype(F32),
                _pad_lanes(_row(gdn_a_log[j])), _pad_lanes(_row(gdn_dt_bias[j])))
            o = _gdn_scan(q, k, v, gc, beta)
            h = _gdn_out(h, o.reshape(m, GDN_V_DIM), wz, _row(gdn_norm_w[j]),
                         gdn_w_out[j].astype(BF16), _row(ln_g[i, 1]), _row(ln_b[i, 1]))
        h = _ffn_block(h, ffn_post_w_in[i].astype(BF16), ffn_post_w_out[i].astype(BF16),
                       _row(ln_g[i, 2]), _row(ln_b[i, 2]))
    return h.reshape(batch, seq, d).astype(x.dtype)
```

```python
import functools
import math

import jax
import jax.numpy as jnp
from jax import lax
from jax.experimental import pallas as pl
from jax.experimental.pallas import tpu as pltpu

F32 = jnp.float32
BF16 = jnp.bfloat16

D_MODEL = 1024
DEPTH = 2
D_FF = 2816
HEAD_DIM = 128
MOBA_HEADS = D_MODEL // HEAD_DIM
MOBA_BLOCK = 256
MOBA_TOPK = 3
ROPE_THETA = 10000.0
GDN_QK_HEADS = D_MODEL // HEAD_DIM
GDN_V_HEADS = 2 * GDN_QK_HEADS
GDN_QK_DIM = GDN_QK_HEADS * HEAD_DIM
GDN_V_DIM = GDN_V_HEADS * HEAD_DIM
GDN_CONV_DIM = 2 * GDN_QK_DIM + GDN_V_DIM
GDN_CONV = 4
GDN_CHUNK = 64
ALPHA = (2 * DEPTH) ** 0.25
HALF_STEP = 0.5
LN_EPS = 1e-5
RMS_EPS = 1e-6
NEG_INF = -1e30

LANES = 128
SUBLANES = 8
VMEM_LIMIT = 56 * 1024 * 1024

NT_DIMS = (((1,), (1,)), ((), ()))
TN_DIMS = (((0,), (0,)), ((), ()))


def _dot(a, b):
    return jnp.dot(a, b, preferred_element_type=F32)


def _dot_nt(a, b):
    return lax.dot_general(a, b, NT_DIMS, preferred_element_type=F32)


def _sigmoid(x):
    return 1.0 / (1.0 + jnp.exp(-x))


def _layer_norm(y, g, b):
    mu = jnp.mean(y, axis=-1, keepdims=True)
    d = y - mu
    var = jnp.mean(d * d, axis=-1, keepdims=True)
    return d * lax.rsqrt(var + LN_EPS) * g + b


def _resident(shape):
    nd = len(shape)
    return pl.BlockSpec(shape, lambda *_: (0,) * nd, pipeline_mode=pl.Buffered(1))


def _params(semantics):
    return pltpu.CompilerParams(dimension_semantics=semantics, vmem_limit_bytes=VMEM_LIMIT)


FFN_ROWS = 512
FFN_CHUNK = 1408


def _ffn_kernel(h_ref, win_ref, wout_ref, g_ref, b_ref, o_ref):
    h = h_ref[...]
    hb = h.astype(BF16)
    acc = None
    for j in range(D_FF // FFN_CHUNK):
        lo = j * FFN_CHUNK
        gate = _dot(hb, win_ref[:, lo:lo + FFN_CHUNK])
        up = _dot(hb, win_ref[:, D_FF + lo:D_FF + lo + FFN_CHUNK])
        act = (gate * _sigmoid(gate) * up).astype(BF16)
        part = _dot(act, wout_ref[lo:lo + FFN_CHUNK, :])
        acc = part if acc is None else acc + part
    y = ALPHA * h + HALF_STEP * acc
    o_ref[...] = _layer_norm(y, g_ref[...], b_ref[...])


def _ffn_block(h, w_in, w_out, g, b):
    m, d = h.shape
    return pl.pallas_call(
        _ffn_kernel,
        grid=(m // FFN_ROWS,),
        in_specs=[
            pl.BlockSpec((FFN_ROWS, d), lambda i: (i, 0)),
            _resident(w_in.shape),
            _resident(w_out.shape),
            _resident(g.shape),
            _resident(b.shape),
        ],
        out_specs=pl.BlockSpec((FFN_ROWS, d), lambda i: (i, 0)),
        out_shape=jax.ShapeDtypeStruct((m, d), F32),
        compiler_params=_params(("parallel",)),
        name="ffn_ln",
    )(h, w_in, w_out, g, b)


PROJ_ROWS = 512


def _proj_ln_kernel(h_ref, x_ref, w_ref, g_ref, b_ref, o_ref):
    y = ALPHA * h_ref[...] + _dot(x_ref[...], w_ref[...])
    o_ref[...] = _layer_norm(y, g_ref[...], b_ref[...])


def _proj_ln(h, x, w, g, b):
    m, d = h.shape
    return pl.pallas_call(
        _proj_ln_kernel,
        grid=(m // PROJ_ROWS,),
        in_specs=[
            pl.BlockSpec((PROJ_ROWS, d), lambda i: (i, 0)),
            pl.BlockSpec((PROJ_ROWS, x.shape[1]), lambda i: (i, 0)),
            _resident(w.shape),
            _resident(g.shape),
            _resident(b.shape),
        ],
        out_specs=pl.BlockSpec((PROJ_ROWS, d), lambda i: (i, 0)),
        out_shape=jax.ShapeDtypeStruct((m, d), F32),
        compiler_params=_params(("parallel",)),
        name="proj_ln",
    )(h, x, w, g, b)


QKV_ROWS = 512


def _qkv_rope_kernel(h_ref, wqk_ref, wvt_ref, cos_ref, sin_ref, q_ref, k_ref, vt_ref):
    hb = h_ref[...].astype(BF16)
    cos = cos_ref[...]
    sin = sin_ref[...]
    for part, out_ref in ((0, q_ref), (1, k_ref)):
        x = _dot(hb, wqk_ref[:, part * D_MODEL:(part + 1) * D_MODEL])
        for hh in range(MOBA_HEADS):
            xs = x[:, hh * HEAD_DIM:(hh + 1) * HEAD_DIM]
            rot = pltpu.roll(xs, HEAD_DIM // 2, axis=1)
            out_ref[:, hh * HEAD_DIM:(hh + 1) * HEAD_DIM] = (xs * cos + rot * sin).astype(BF16)
    vt = _dot_nt(wvt_ref[...], hb).astype(BF16)
    for hh in range(MOBA_HEADS):
        for blk in range(QKV_ROWS // MOBA_BLOCK):
            vt_ref[hh, blk] = vt[hh * HEAD_DIM:(hh + 1) * HEAD_DIM, blk * MOBA_BLOCK:(blk + 1) * MOBA_BLOCK]


def _rope_tables(seq):
    half = HEAD_DIM // 2
    inv_freq = ROPE_THETA ** (-jnp.arange(half, dtype=F32) / half)
    ang = jnp.arange(seq).astype(F32)[:, None] * inv_freq[None, :]
    cos = jnp.cos(ang)
    sin = jnp.sin(ang)
    return jnp.concatenate([cos, cos], axis=-1), jnp.concatenate([-sin, sin], axis=-1)


def _qkv_rope(h3, wqk, wvt, cos, sin):
    batch, seq, d = h3.shape
    tm = QKV_ROWS
    blocks = tm // MOBA_BLOCK
    row_spec = pl.BlockSpec((None, tm, d), lambda b, t: (b, t, 0))
    tab_spec = pl.BlockSpec((tm, HEAD_DIM), lambda b, t: (t, 0))
    vt_spec = pl.BlockSpec((None, MOBA_HEADS, blocks, HEAD_DIM, MOBA_BLOCK), lambda b, t: (b, 0, t, 0, 0))
    qk_out = jax.ShapeDtypeStruct((batch, seq, d), BF16)
    vt_out = jax.ShapeDtypeStruct((batch, MOBA_HEADS, seq // MOBA_BLOCK, HEAD_DIM, MOBA_BLOCK), BF16)
    return pl.pallas_call(
        _qkv_rope_kernel,
        grid=(batch, seq // tm),
        in_specs=[row_spec, _resident(wqk.shape), _resident(wvt.shape), tab_spec, tab_spec],
        out_specs=[row_spec, row_spec, vt_spec],
        out_shape=[qk_out, qk_out, vt_out],
        compiler_params=_params(("parallel", "parallel")),
        name="moba_qkv_rope",
    )(h3, wqk, wvt, cos, sin)


MOBA_GROUP = 4


def _moba_attn_kernel(q_ref, k_ref, vt_ref, o_ref, kmean_ref, bias_ref, m_ref, l_ref, acc_ref,
                      *, n_blk, n_pad):
    qi = pl.program_id(2)
    bs = MOBA_BLOCK
    c_exp = (HEAD_DIM ** -0.5) * math.log2(math.e)

    @pl.when(qi == 0)
    def _():
        kmean_ref[...] = jnp.zeros_like(kmean_ref)
        for n in range(n_blk):
            kb = k_ref[n * bs:(n + 1) * bs, :].astype(F32)
            kmean_ref[n:n + 1, :] = jnp.sum(kb, axis=0, keepdims=True) * (1.0 / bs)

    q = q_ref[...]

    gate = _dot_nt(kmean_ref[...].astype(BF16), q)
    blk = lax.broadcasted_iota(jnp.int32, gate.shape, 0)
    blk_f = blk.astype(F32)
    past = blk < qi
    gate = jnp.where(past, gate, NEG_INF)
    sel = jnp.zeros(gate.shape, F32)
    for _ in range(min(MOBA_TOPK, n_blk)):
        top = jnp.max(gate, axis=0, keepdims=True)
        first = jnp.min(jnp.where(gate == top, blk_f, float(n_pad)), axis=0, keepdims=True)
        pick = blk_f == first
        sel = jnp.where(jnp.logical_and(pick, past), 1.0, sel)
        gate = jnp.where(pick, -jnp.inf, gate)
    bias = jnp.where(sel > 0.5, 0.0, NEG_INF)
    for n in range(n_blk):
        bias_ref[n] = jnp.broadcast_to(bias[n:n + 1, :], (SUBLANES, bs))

    own = pl.multiple_of(qi * bs, bs)
    s = _dot_nt(k_ref[pl.ds(own, bs), :], q)
    key = lax.broadcasted_iota(jnp.int32, s.shape, 0)
    qry = lax.broadcasted_iota(jnp.int32, s.shape, 1)
    s = jnp.where(key <= qry, s, NEG_INF)
    m0 = jnp.max(s, axis=0, keepdims=True)
    p = jnp.exp2((s - m0) * c_exp)
    m_ref[...] = m0
    l_ref[...] = jnp.sum(p, axis=0, keepdims=True)
    acc_ref[...] = _dot(vt_ref[qi], p.astype(BF16))

    def past_group(gi, carry):
        n0 = gi * MOBA_GROUP
        off = pl.multiple_of(n0 * bs, MOBA_GROUP * bs)
        s_all = _dot_nt(k_ref[pl.ds(off, MOBA_GROUP * bs), :], q)
        parts = [s_all[g * bs:(g + 1) * bs] + bias_ref[n0 + g][0:1, :] for g in range(MOBA_GROUP)]
        m_old = m_ref[...]
        m_new = m_old
        for part in parts:
            m_new = jnp.maximum(m_new, jnp.max(part, axis=0, keepdims=True))
        corr = jnp.exp2((m_old - m_new) * c_exp)
        l_new = l_ref[...] * corr
        acc = acc_ref[...] * corr
        for g, part in enumerate(parts):
            p = jnp.exp2((part - m_new) * c_exp)
            l_new = l_new + jnp.sum(p, axis=0, keepdims=True)
            acc = acc + _dot(vt_ref[n0 + g], p.astype(BF16))
        m_ref[...] = m_new
        l_ref[...] = l_new
        acc_ref[...] = acc
        return carry

    lax.fori_loop(0, (qi + MOBA_GROUP - 1) // MOBA_GROUP, past_group, 0)
    o_ref[...] = (acc_ref[...] / l_ref[...]).T.astype(o_ref.dtype)


def _moba_attention(q, k, vt):
    batch, seq, _ = q.shape
    n_blk = seq // MOBA_BLOCK
    n_pad = -(-n_blk // SUBLANES) * SUBLANES
    assert seq % MOBA_BLOCK == 0 and n_blk % MOBA_GROUP == 0
    q_spec = pl.BlockSpec((None, MOBA_BLOCK, HEAD_DIM), lambda b, h, i: (b, i, h))
    k_spec = pl.BlockSpec((None, seq, HEAD_DIM), lambda b, h, i: (b, 0, h))
    vt_spec = pl.BlockSpec((None, None, n_blk, HEAD_DIM, MOBA_BLOCK), lambda b, h, i: (b, h, 0, 0, 0))
    return pl.pallas_call(
        functools.partial(_moba_attn_kernel, n_blk=n_blk, n_pad=n_pad),
        grid=(batch, MOBA_HEADS, n_blk),
        in_specs=[q_spec, k_spec, vt_spec],
        out_specs=q_spec,
        out_shape=jax.ShapeDtypeStruct(q.shape, BF16),
        scratch_shapes=[
            pltpu.VMEM((n_pad, HEAD_DIM), F32),
            pltpu.VMEM((n_blk, SUBLANES, MOBA_BLOCK), F32),
            pltpu.VMEM((1, MOBA_BLOCK), F32),
            pltpu.VMEM((1, MOBA_BLOCK), F32),
            pltpu.VMEM((HEAD_DIM, MOBA_BLOCK), F32),
        ],
        compiler_params=_params(("parallel", "parallel", "arbitrary")),
        name="moba_attn",
    )(q, k, vt)


GDN_IN_ROWS = 256


def _gdn_in_kernel(h_ref, wqkv_ref, wb_ref, wa_ref, cw_ref, alog_ref, dtb_ref,
                   q_ref, k_ref, v_ref, beta_ref, gc_ref, xs_ref):
    tm = GDN_IN_ROWS
    halo = SUBLANES
    hb = h_ref[...].astype(BF16)

    @pl.when(pl.program_id(1) == 0)
    def _():
        xs_ref[0:halo, :] = jnp.zeros((halo, GDN_CONV_DIM), F32)

    xs_ref[halo:halo + tm, :] = _dot(hb, wqkv_ref[...])

    n_cols = GDN_CONV_DIM // HEAD_DIM
    for c in range(n_cols):
        cs = slice(c * HEAD_DIM, (c + 1) * HEAD_DIM)
        y = None
        for i in range(GDN_CONV):
            r0 = halo - GDN_CONV + 1 + i
            term = xs_ref[r0:r0 + tm, cs] * cw_ref[i:i + 1, cs]
            y = term if y is None else y + term
        y = y * _sigmoid(y)
        if c < 2 * GDN_QK_HEADS:
            y = y * lax.rsqrt(jnp.sum(y * y, axis=-1, keepdims=True) + RMS_EPS)
            if c < GDN_QK_HEADS:
                q_ref[c] = (y * (HEAD_DIM ** -0.5)).astype(BF16)
            else:
                k_ref[c - GDN_QK_HEADS] = y.astype(BF16)
        else:
            v_ref[c - 2 * GDN_QK_HEADS] = y.astype(BF16)

    xs_ref[0:halo, :] = xs_ref[tm:tm + halo, :]

    beta = _sigmoid(_dot(hb, wb_ref[...]))
    a = _dot(hb, wa_ref[...]) + dtb_ref[...]
    softplus = jnp.maximum(a, 0.0) + jnp.log1p(jnp.exp(-jnp.abs(a)))
    g = -jnp.exp(alog_ref[...]) * softplus
    ri = lax.broadcasted_iota(jnp.int32, (tm, tm), 0)
    ci = lax.broadcasted_iota(jnp.int32, (tm, tm), 1)
    tri = jnp.logical_and(ri >= ci, ri // GDN_CHUNK == ci // GDN_CHUNK).astype(F32)
    gc = jnp.dot(tri, g, preferred_element_type=F32, precision=lax.Precision.HIGHEST)
    beta_ref[...] = beta[:, :GDN_V_HEADS]
    gc_ref[...] = gc[:, :GDN_V_HEADS]


def _gdn_in(h3, wqkv, wb, wa, conv_w, alog, dtb):
    batch, seq, d = h3.shape
    tm = GDN_IN_ROWS

    def rows(width):
        return pl.BlockSpec((None, tm, width), lambda b, t: (b, t, 0))

    def heads(n):
        return pl.BlockSpec((None, n, tm, HEAD_DIM), lambda b, t: (b, 0, t, 0))

    def head_out(n):
        return jax.ShapeDtypeStruct((batch, n, seq, HEAD_DIM), BF16)

    gate_out = jax.ShapeDtypeStruct((batch, seq, GDN_V_HEADS), F32)
    return pl.pallas_call(
        _gdn_in_kernel,
        grid=(batch, seq // tm),
        in_specs=[rows(d), _resident(wqkv.shape), _resident(wb.shape), _resident(wa.shape),
                  _resident(conv_w.shape), _resident(alog.shape), _resident(dtb.shape)],
        out_specs=[heads(GDN_QK_HEADS), heads(GDN_QK_HEADS), heads(GDN_V_HEADS),
                   rows(GDN_V_HEADS), rows(GDN_V_HEADS)],
        out_shape=[head_out(GDN_QK_HEADS), head_out(GDN_QK_HEADS), head_out(GDN_V_HEADS),
                   gate_out, gate_out],
        scratch_shapes=[pltpu.VMEM((tm + SUBLANES, GDN_CONV_DIM), F32)],
        compiler_params=_params(("parallel", "arbitrary")),
        name="gdn_in",
    )(h3, wqkv, wb, wa, conv_w, alog, dtb)


GDN_TILE = 256
GDN_KHEADS_PER_ITER = 2


def _gdn_core_kernel(q_ref, k_ref, v_ref, gcol_ref, bcol_ref, grow_ref, o_ref,
                     s_ref, u_ref, wq_ref, kd_ref, attn_ref, gl_ref, bd_ref):
    tile = GDN_TILE
    c_len = GDN_CHUNK
    n_chunks = tile // c_len
    group = GDN_V_HEADS // GDN_QK_HEADS

    @pl.when(pl.program_id(1) == 0)
    def _():
        s_ref[...] = jnp.zeros_like(s_ref)

    ii = lax.broadcasted_iota(jnp.int32, (tile, tile), 0)
    jj = lax.broadcasted_iota(jnp.int32, (tile, tile), 1)
    same_chunk = (ii // c_len) == (jj // c_len)
    bd_lower = jnp.logical_and(same_chunk, ii >= jj)
    bd_strict = jnp.logical_and(same_chunk, ii > jj)
    bd_ref[...] = same_chunk.astype(BF16)
    ci = lax.broadcasted_iota(jnp.int32, (c_len, tile), 0)
    cj = lax.broadcasted_iota(jnp.int32, (c_len, tile), 1)
    eye_cat = (ci == cj % c_len).astype(F32)
    head_lane = lax.broadcasted_iota(jnp.int32, (tile, GDN_V_HEADS), 1)

    def collapse(x_bd):
        out = x_bd[0:c_len]
        for c in range(1, n_chunks):
            out = out + x_bd[c * c_len:(c + 1) * c_len]
        return out

    def expand(x_cat):
        return jnp.concatenate([x_cat] * n_chunks, axis=0) * bd_ref[...]

    def prepare(hk, q_t, k_t, kk, qk, j):
        hv = hk * group + j
        pick = head_lane == hv
        gcc = jnp.sum(jnp.where(pick, gcol_ref[...], 0.0), axis=-1, keepdims=True)
        beta = jnp.sum(jnp.where(pick, bcol_ref[...], 0.0), axis=-1, keepdims=True)
        gcr = grow_ref[hv]
        decay = jnp.exp(jnp.where(bd_lower, gcc - gcr, NEG_INF))
        attn_bd = (qk * decay).astype(BF16)
        for c in range(n_chunks):
            attn_ref[hv, c] = attn_bd[c * c_len:(c + 1) * c_len, c * c_len:(c + 1) * c_len]
        p_bd32 = jnp.where(bd_strict, -(beta * kk * decay), 0.0)
        gam = jnp.exp(gcc)
        k32 = k_t.astype(F32)
        v32 = v_ref[hv].astype(F32)
        rhs = jnp.concatenate([v32 * beta, k32 * (beta * gam)], axis=1).astype(BF16)
        q_dec = (q_t.astype(F32) * gam).astype(BF16)
        g_last = jnp.concatenate(
            [jnp.broadcast_to(gcc[(c + 1) * c_len - 1:(c + 1) * c_len], (c_len, 1)) for c in range(n_chunks)],
            axis=0)
        kd_ref[hv] = (k32 * jnp.exp(g_last - gcc)).astype(BF16)
        for c in range(n_chunks):
            wq_ref[hv, (2 * c + 1) * c_len:(2 * c + 2) * c_len] = q_dec[c * c_len:(c + 1) * c_len]
            gl_ref[hv, c] = jnp.broadcast_to(jnp.exp(g_last[c * c_len:c * c_len + 1]), (SUBLANES, LANES))
        p_cat = collapse(p_bd32)
        return dict(hv=hv, p_bd=p_bd32.astype(BF16), p_cat=p_cat, t_cat=eye_cat + p_cat, rhs=rhs)

    def phase1(it, carry):
        hs = []
        for kh in range(GDN_KHEADS_PER_ITER):
            hk = it * GDN_KHEADS_PER_ITER + kh
            q_t = q_ref[hk]
            k_t = k_ref[hk]
            kk = _dot_nt(k_t, k_t)
            qk = _dot_nt(q_t, k_t)
            for j in range(group):
                hs.append(prepare(hk, q_t, k_t, kk, qk, j))
        for h in hs:
            h["p_cat"] = _dot(h["p_cat"].astype(BF16), h["p_bd"])
        n = 2
        while 2 * n < c_len:
            for h in hs:
                lhs = jnp.concatenate([h["t_cat"], h["p_cat"]], axis=0).astype(BF16)
                both = _dot(lhs, expand(h["p_cat"].astype(BF16)))
                h["t_cat"] = h["t_cat"] + both[:c_len]
                h["p_cat"] = both[c_len:]
            n *= 2
        for h in hs:
            h["t_cat"] = h["t_cat"] + _dot(h["t_cat"].astype(BF16), expand(h["p_cat"].astype(BF16)))
        for h in hs:
            uw = _dot(expand(h["t_cat"].astype(BF16)), h["rhs"])
            hv = h["hv"]
            u_ref[hv] = uw[:, :HEAD_DIM]
            w_b = uw[:, HEAD_DIM:].astype(BF16)
            for c in range(n_chunks):
                wq_ref[hv, 2 * c * c_len:(2 * c + 1) * c_len] = w_b[c * c_len:(c + 1) * c_len]
        return carry

    lax.fori_loop(0, GDN_QK_HEADS // GDN_KHEADS_PER_ITER, phase1, 0)

    def phase2(c, carry):
        r0 = pl.multiple_of(c * c_len, c_len)
        r2 = pl.multiple_of(c * 2 * c_len, 2 * c_len)
        boths = [_dot(wq_ref[hv, pl.ds(r2, 2 * c_len), :], s_ref[hv].astype(BF16))
                 for hv in range(GDN_V_HEADS)]
        for hv in range(GDN_V_HEADS):
            v_new = (u_ref[hv, pl.ds(r0, c_len), :] - boths[hv][:c_len]).astype(BF16)
            o_ref[pl.ds(r0, c_len), hv * HEAD_DIM:(hv + 1) * HEAD_DIM] = (
                boths[hv][c_len:] + _dot(attn_ref[hv, c], v_new))
            s_ref[hv] = s_ref[hv] * gl_ref[hv, c][0:1, :] + lax.dot_general(
                kd_ref[hv, pl.ds(r0, c_len), :], v_new, TN_DIMS, preferred_element_type=F32)
        return carry

    lax.fori_loop(0, n_chunks, phase2, 0)


def _gdn_core(q, k, v, gc, beta):
    batch, _, seq, _ = q.shape
    tile = GDN_TILE
    n_chunks = tile // GDN_CHUNK
    nh = GDN_V_HEADS
    g_row = gc.transpose(0, 2, 1).reshape(batch, nh, seq // tile, 1, tile)

    def heads(n):
        return pl.BlockSpec((None, n, tile, HEAD_DIM), lambda b, t: (b, 0, t, 0))

    col_spec = pl.BlockSpec((None, tile, nh), lambda b, t: (b, t, 0))
    row_spec = pl.BlockSpec((None, nh, None, 1, tile), lambda b, t: (b, 0, t, 0, 0))
    return pl.pallas_call(
        _gdn_core_kernel,
        grid=(batch, seq // tile),
        in_specs=[heads(GDN_QK_HEADS), heads(GDN_QK_HEADS), heads(nh), col_spec, col_spec, row_spec],
        out_specs=pl.BlockSpec((None, tile, GDN_V_DIM), lambda b, t: (b, t, 0)),
        out_shape=jax.ShapeDtypeStruct((batch, seq, GDN_V_DIM), F32),
        scratch_shapes=[
            pltpu.VMEM((nh, HEAD_DIM, HEAD_DIM), F32),
            pltpu.VMEM((nh, tile, HEAD_DIM), F32),
            pltpu.VMEM((nh, 2 * tile, HEAD_DIM), BF16),
            pltpu.VMEM((nh, tile, HEAD_DIM), BF16),
            pltpu.VMEM((nh, n_chunks, GDN_CHUNK, GDN_CHUNK), BF16),
            pltpu.VMEM((nh, n_chunks, SUBLANES, LANES), F32),
            pltpu.VMEM((tile, tile), BF16),
        ],
        compiler_params=_params(("parallel", "arbitrary")),
        name="gdn_core",
    )(q, k, v, gc, beta, g_row)


GDN_OUT_ROWS = 256


def _gdn_out_kernel(h_ref, o_ref, wz_ref, nw_ref, wo_ref, g_ref, b_ref, out_ref, y_ref):
    h = h_ref[...]
    z = _dot(h.astype(BF16), wz_ref[...])
    nw = nw_ref[...]
    for n in range(GDN_V_HEADS):
        cs = slice(n * HEAD_DIM, (n + 1) * HEAD_DIM)
        o = o_ref[:, cs]
        zz = z[:, cs]
        y = o * lax.rsqrt(jnp.mean(o * o, axis=-1, keepdims=True) + RMS_EPS) * nw * (zz * _sigmoid(zz))
        y_ref[:, cs] = y.astype(BF16)
    y = ALPHA * h + _dot(y_ref[...], wo_ref[...])
    out_ref[...] = _layer_norm(y, g_ref[...], b_ref[...])


def _gdn_out(h, o, wz, nw, wo, g, b):
    m, d = h.shape
    tm = GDN_OUT_ROWS
    return pl.pallas_call(
        _gdn_out_kernel,
        grid=(m // tm,),
        in_specs=[pl.BlockSpec((tm, d), lambda i: (i, 0)),
                  pl.BlockSpec((tm, GDN_V_DIM), lambda i: (i, 0)),
                  _resident(wz.shape), _resident(nw.shape), _resident(wo.shape),
                  _resident(g.shape), _resident(b.shape)],
        out_specs=pl.BlockSpec((tm, d), lambda i: (i, 0)),
        out_shape=jax.ShapeDtypeStruct((m, d), F32),
        scratch_shapes=[pltpu.VMEM((tm, GDN_V_DIM), BF16)],
        compiler_params=_params(("parallel",)),
        name="gdn_out",
    )(h, o, wz, nw, wo, g, b)


def _row(vec):
    return vec.reshape(1, -1).astype(F32)


def _pad_lanes(x2d):
    return jnp.pad(x2d, ((0, 0), (0, LANES - x2d.shape[1])))


def kernel(x, ln_g, ln_b, ffn_pre_w_in, ffn_pre_w_out, ffn_post_w_in, ffn_post_w_out,
           moba_w_in, moba_w_out, gdn_w_in, gdn_conv_w, gdn_a_log, gdn_dt_bias,
           gdn_norm_w, gdn_w_out):
    batch, seq, d = x.shape
    m = batch * seq
    h = x.reshape(m, d).astype(F32)
    cos, sin = _rope_tables(seq)

    for i in range(DEPTH):
        h = _ffn_block(h, ffn_pre_w_in[i].astype(BF16), ffn_pre_w_out[i].astype(BF16),
                       _row(ln_g[i, 0]), _row(ln_b[i, 0]))
        j = i // 2
        if i % 2 == 0:
            w_in = moba_w_in[j]
            q, k, vt = _qkv_rope(h.reshape(batch, seq, d), w_in[:, :2 * D_MODEL].astype(BF16),
                                 w_in[:, 2 * D_MODEL:].T.astype(BF16), cos, sin)
            o = _moba_attention(q, k, vt)
            h = _proj_ln(h, o.reshape(m, d), moba_w_out[j].astype(BF16),
                         _row(ln_g[i, 1]), _row(ln_b[i, 1]))
        else:
            w_in = gdn_w_in[j]
            z0 = GDN_CONV_DIM
            b0 = z0 + GDN_V_DIM
            a0 = b0 + GDN_V_HEADS
            wqkv = w_in[:, :z0].astype(BF16)
            wz = w_in[:, z0:b0].astype(BF16)
            wb = _pad_lanes(w_in[:, b0:a0]).astype(BF16)
            wa = _pad_lanes(w_in[:, a0:a0 + GDN_V_HEADS]).astype(BF16)
            q, k, v, beta, gc = _gdn_in(
                h.reshape(batch, seq, d), wqkv, wb, wa, gdn_conv_w[j].astype(F32),
                _pad_lanes(_row(gdn_a_log[j])), _pad_lanes(_row(gdn_dt_bias[j])))
            o = _gdn_core(q, k, v, gc, beta)
            h = _gdn_out(h, o.reshape(m, GDN_V_DIM), wz, _row(gdn_norm_w[j]),
                         gdn_w_out[j].astype(BF16), _row(ln_g[i, 1]), _row(ln_b[i, 1]))
        h = _ffn_block(h, ffn_post_w_in[i].astype(BF16), ffn_post_w_out[i].astype(BF16),
                       _row(ln_g[i, 2]), _row(ln_b[i, 2]))
    return h.reshape(batch, seq, d).astype(x.dtype)
```

```python
import functools
import math

import jax
import jax.numpy as jnp
from jax import lax
from jax.experimental import pallas as pl
from jax.experimental.pallas import tpu as pltpu

F32 = jnp.float32
BF16 = jnp.bfloat16

D_MODEL = 1024
DEPTH = 2
D_FF = 2816
HEAD_DIM = 128
MOBA_HEADS = D_MODEL // HEAD_DIM
MOBA_BLOCK = 256
MOBA_TOPK = 3
ROPE_THETA = 10000.0
GDN_QK_HEADS = D_MODEL // HEAD_DIM
GDN_V_HEADS = 2 * GDN_QK_HEADS
GDN_QK_DIM = GDN_QK_HEADS * HEAD_DIM
GDN_V_DIM = GDN_V_HEADS * HEAD_DIM
GDN_CONV_DIM = 2 * GDN_QK_DIM + GDN_V_DIM
GDN_CONV = 4
GDN_CHUNK = 64
ALPHA = (2 * DEPTH) ** 0.25
HALF_STEP = 0.5
LN_EPS = 1e-5
RMS_EPS = 1e-6
NEG_INF = -1e30

LANES = 128
SUBLANES = 8
VMEM_LIMIT = 56 * 1024 * 1024

NT_DIMS = (((1,), (1,)), ((), ()))
TN_DIMS = (((0,), (0,)), ((), ()))


def _dot(a, b):
    return jnp.dot(a, b, preferred_element_type=F32)


def _dot_nt(a, b):
    return lax.dot_general(a, b, NT_DIMS, preferred_element_type=F32)


def _sigmoid(x):
    return 1.0 / (1.0 + jnp.exp(-x))


def _layer_norm(y, g, b):
    mu = jnp.mean(y, axis=-1, keepdims=True)
    d = y - mu
    var = jnp.mean(d * d, axis=-1, keepdims=True)
    return d * lax.rsqrt(var + LN_EPS) * g + b


def _resident(shape):
    nd = len(shape)
    return pl.BlockSpec(shape, lambda *_: (0,) * nd, pipeline_mode=pl.Buffered(1))


def _params(semantics):
    return pltpu.CompilerParams(dimension_semantics=semantics, vmem_limit_bytes=VMEM_LIMIT)


FFN_ROWS = 512
FFN_CHUNK = 1408


def _ffn_kernel(h_ref, win_ref, wout_ref, g_ref, b_ref, o_ref):
    h = h_ref[...]
    hb = h.astype(BF16)
    acc = None
    for j in range(D_FF // FFN_CHUNK):
        lo = j * FFN_CHUNK
        gate = _dot(hb, win_ref[:, lo:lo + FFN_CHUNK])
        up = _dot(hb, win_ref[:, D_FF + lo:D_FF + lo + FFN_CHUNK])
        act = (gate * _sigmoid(gate) * up).astype(BF16)
        part = _dot(act, wout_ref[lo:lo + FFN_CHUNK, :])
        acc = part if acc is None else acc + part
    y = ALPHA * h + HALF_STEP * acc
    o_ref[...] = _layer_norm(y, g_ref[...], b_ref[...])


def _ffn_block(h, w_in, w_out, g, b):
    m, d = h.shape
    return pl.pallas_call(
        _ffn_kernel,
        grid=(m // FFN_ROWS,),
        in_specs=[
            pl.BlockSpec((FFN_ROWS, d), lambda i: (i, 0)),
            _resident(w_in.shape),
            _resident(w_out.shape),
            _resident(g.shape),
            _resident(b.shape),
        ],
        out_specs=pl.BlockSpec((FFN_ROWS, d), lambda i: (i, 0)),
        out_shape=jax.ShapeDtypeStruct((m, d), F32),
        compiler_params=_params(("parallel",)),
        name="ffn_ln",
    )(h, w_in, w_out, g, b)


PROJ_ROWS = 512


def _proj_ln_kernel(h_ref, x_ref, w_ref, g_ref, b_ref, o_ref):
    y = ALPHA * h_ref[...] + _dot(x_ref[...], w_ref[...])
    o_ref[...] = _layer_norm(y, g_ref[...], b_ref[...])


def _proj_ln(h, x, w, g, b):
    m, d = h.shape
    return pl.pallas_call(
        _proj_ln_kernel,
        grid=(m // PROJ_ROWS,),
        in_specs=[
            pl.BlockSpec((PROJ_ROWS, d), lambda i: (i, 0)),
            pl.BlockSpec((PROJ_ROWS, x.shape[1]), lambda i: (i, 0)),
            _resident(w.shape),
            _resident(g.shape),
            _resident(b.shape),
        ],
        out_specs=pl.BlockSpec((PROJ_ROWS, d), lambda i: (i, 0)),
        out_shape=jax.ShapeDtypeStruct((m, d), F32),
        compiler_params=_params(("parallel",)),
        name="proj_ln",
    )(h, x, w, g, b)


QKV_ROWS = 512


def _qkv_rope_kernel(h_ref, wqk_ref, wvt_ref, cos_ref, sin_ref, q_ref, k_ref, vt_ref):
    hb = h_ref[...].astype(BF16)
    cos = cos_ref[...]
    sin = sin_ref[...]
    for part, out_ref in ((0, q_ref), (1, k_ref)):
        x = _dot(hb, wqk_ref[:, part * D_MODEL:(part + 1) * D_MODEL])
        for hh in range(MOBA_HEADS):
            xs = x[:, hh * HEAD_DIM:(hh + 1) * HEAD_DIM]
            rot = pltpu.roll(xs, HEAD_DIM // 2, axis=1)
            out_ref[:, hh * HEAD_DIM:(hh + 1) * HEAD_DIM] = (xs * cos + rot * sin).astype(BF16)
    vt = _dot_nt(wvt_ref[...], hb).astype(BF16)
    for hh in range(MOBA_HEADS):
        for blk in range(QKV_ROWS // MOBA_BLOCK):
            vt_ref[hh, blk] = vt[hh * HEAD_DIM:(hh + 1) * HEAD_DIM, blk * MOBA_BLOCK:(blk + 1) * MOBA_BLOCK]


def _rope_tables(seq):
    half = HEAD_DIM // 2
    inv_freq = ROPE_THETA ** (-jnp.arange(half, dtype=F32) / half)
    ang = jnp.arange(seq).astype(F32)[:, None] * inv_freq[None, :]
    cos = jnp.cos(ang)
    sin = jnp.sin(ang)
    return jnp.concatenate([cos, cos], axis=-1), jnp.concatenate([-sin, sin], axis=-1)


def _qkv_rope(h3, wqk, wvt, cos, sin):
    batch, seq, d = h3.shape
    tm = QKV_ROWS
    blocks = tm // MOBA_BLOCK
    row_spec = pl.BlockSpec((None, tm, d), lambda b, t: (b, t, 0))
    tab_spec = pl.BlockSpec((tm, HEAD_DIM), lambda b, t: (t, 0))
    vt_spec = pl.BlockSpec((None, MOBA_HEADS, blocks, HEAD_DIM, MOBA_BLOCK), lambda b, t: (b, 0, t, 0, 0))
    qk_out = jax.ShapeDtypeStruct((batch, seq, d), BF16)
    vt_out = jax.ShapeDtypeStruct((batch, MOBA_HEADS, seq // MOBA_BLOCK, HEAD_DIM, MOBA_BLOCK), BF16)
    return pl.pallas_call(
        _qkv_rope_kernel,
        grid=(batch, seq // tm),
        in_specs=[row_spec, _resident(wqk.shape), _resident(wvt.shape), tab_spec, tab_spec],
        out_specs=[row_spec, row_spec, vt_spec],
        out_shape=[qk_out, qk_out, vt_out],
        compiler_params=_params(("parallel", "parallel")),
        name="moba_qkv_rope",
    )(h3, wqk, wvt, cos, sin)


def _moba_attn_kernel(q_ref, k_ref, vt_ref, o_ref, kmean_ref, *, n_blk, n_pad):
    qi = pl.program_id(2)
    bs = MOBA_BLOCK
    c_exp = (HEAD_DIM ** -0.5) * math.log2(math.e)

    @pl.when(qi == 0)
    def _():
        kmean_ref[...] = jnp.zeros_like(kmean_ref)
        for n in range(n_blk):
            kb = k_ref[n * bs:(n + 1) * bs, :].astype(F32)
            kmean_ref[n:n + 1, :] = jnp.sum(kb, axis=0, keepdims=True) * (1.0 / bs)

    def tile(n_past):
        q = q_ref[...]
        if n_past > 0:
            gate = _dot_nt(kmean_ref[...].astype(BF16), q)
        s = _dot_nt(k_ref[n_past * bs:(n_past + 1) * bs, :], q) * c_exp
        if n_past > 0:
            s_past = _dot_nt(k_ref[0:n_past * bs, :], q) * c_exp

        key = lax.broadcasted_iota(jnp.int32, s.shape, 0)
        qry = lax.broadcasted_iota(jnp.int32, s.shape, 1)
        s = jnp.where(key <= qry, s, NEG_INF)
        m = jnp.max(s, axis=0, keepdims=True)
        p = jnp.exp2(s - m)
        l = jnp.sum(p, axis=0, keepdims=True)
        acc = _dot(vt_ref[n_past], p.astype(BF16))

        if n_past > 0:
            blk_f = lax.broadcasted_iota(jnp.int32, gate.shape, 0).astype(F32)
            past = blk_f < float(n_past)
            gate = jnp.where(past, gate, NEG_INF)
            sel = jnp.zeros(gate.shape, F32)
            for _ in range(min(MOBA_TOPK, n_blk)):
                top = jnp.max(gate, axis=0, keepdims=True)
                first = jnp.min(jnp.where(gate == top, blk_f, float(n_pad)), axis=0, keepdims=True)
                pick = blk_f == first
                sel = jnp.where(jnp.logical_and(pick, past), 1.0, sel)
                gate = jnp.where(pick, -jnp.inf, gate)

            for n in range(n_past):
                chosen = sel[n:n + 1, :] > 0.5
                s = s_past[n * bs:(n + 1) * bs]
                m_blk = jnp.where(chosen, jnp.max(s, axis=0, keepdims=True), NEG_INF)
                m_new = jnp.maximum(m, m_blk)
                corr = jnp.exp2(m - m_new)
                p = jnp.exp2(s - jnp.where(chosen, m_new, jnp.inf))
                l = l * corr + jnp.sum(p, axis=0, keepdims=True)
                acc = acc * corr + _dot(vt_ref[n], p.astype(BF16))
                m = m_new

        o_ref[...] = (acc / l).T.astype(o_ref.dtype)

    for n_past in range(n_blk):
        pl.when(qi == n_past)(functools.partial(tile, n_past))


def _moba_attention(q, k, vt):
    batch, seq, _ = q.shape
    n_blk = seq // MOBA_BLOCK
    n_pad = -(-n_blk // SUBLANES) * SUBLANES
    assert seq % MOBA_BLOCK == 0
    q_spec = pl.BlockSpec((None, MOBA_BLOCK, HEAD_DIM), lambda b, h, i: (b, i, h))
    k_spec = pl.BlockSpec((None, seq, HEAD_DIM), lambda b, h, i: (b, 0, h))
    vt_spec = pl.BlockSpec((None, None, n_blk, HEAD_DIM, MOBA_BLOCK), lambda b, h, i: (b, h, 0, 0, 0))
    return pl.pallas_call(
        functools.partial(_moba_attn_kernel, n_blk=n_blk, n_pad=n_pad),
        grid=(batch, MOBA_HEADS, n_blk),
        in_specs=[q_spec, k_spec, vt_spec],
        out_specs=q_spec,
        out_shape=jax.ShapeDtypeStruct(q.shape, BF16),
        scratch_shapes=[
            pltpu.VMEM((n_pad, HEAD_DIM), F32),
        ],
        compiler_params=_params(("parallel", "parallel", "arbitrary")),
        name="moba_attn",
    )(q, k, vt)


GDN_IN_ROWS = 256


def _gdn_in_kernel(h_ref, wqkv_ref, wb_ref, wa_ref, cw_ref, alog_ref, dtb_ref,
                   q_ref, k_ref, v_ref, beta_ref, gc_ref, xs_ref):
    tm = GDN_IN_ROWS
    halo = SUBLANES
    hb = h_ref[...].astype(BF16)

    @pl.when(pl.program_id(1) == 0)
    def _():
        xs_ref[0:halo, :] = jnp.zeros((halo, GDN_CONV_DIM), F32)

    xs_ref[halo:halo + tm, :] = _dot(hb, wqkv_ref[...])

    n_cols = GDN_CONV_DIM // HEAD_DIM
    for c in range(n_cols):
        cs = slice(c * HEAD_DIM, (c + 1) * HEAD_DIM)
        xe = xs_ref[:, cs]
        y = xe[halo:] * cw_ref[GDN_CONV - 1:GDN_CONV, cs]
        for back in range(1, GDN_CONV):
            i = GDN_CONV - 1 - back
            y = y + pltpu.roll(xe, back, axis=0)[halo:] * cw_ref[i:i + 1, cs]
        y = y * _sigmoid(y)
        if c < 2 * GDN_QK_HEADS:
            y = y * lax.rsqrt(jnp.sum(y * y, axis=-1, keepdims=True) + RMS_EPS)
            if c < GDN_QK_HEADS:
                q_ref[c] = (y * (HEAD_DIM ** -0.5)).astype(BF16)
            else:
                k_ref[c - GDN_QK_HEADS] = y.astype(BF16)
        else:
            v_ref[c - 2 * GDN_QK_HEADS] = y.astype(BF16)

    xs_ref[0:halo, :] = xs_ref[tm:tm + halo, :]

    beta = _sigmoid(_dot(hb, wb_ref[...]))
    a = _dot(hb, wa_ref[...]) + dtb_ref[...]
    softplus = jnp.maximum(a, 0.0) + jnp.log1p(jnp.exp(-jnp.abs(a)))
    g = -jnp.exp(alog_ref[...]) * softplus
    ri = lax.broadcasted_iota(jnp.int32, (tm, tm), 0)
    ci = lax.broadcasted_iota(jnp.int32, (tm, tm), 1)
    tri = jnp.logical_and(ri >= ci, ri // GDN_CHUNK == ci // GDN_CHUNK).astype(F32)
    gc = jnp.dot(tri, g, preferred_element_type=F32, precision=lax.Precision.HIGHEST)
    beta_ref[...] = beta[:, :GDN_V_HEADS]
    gc_ref[...] = gc[:, :GDN_V_HEADS]


def _gdn_in(h3, wqkv, wb, wa, conv_w, alog, dtb):
    batch, seq, d = h3.shape
    tm = GDN_IN_ROWS

    def rows(width):
        return pl.BlockSpec((None, tm, width), lambda b, t: (b, t, 0))

    def heads(n):
        return pl.BlockSpec((None, n, tm, HEAD_DIM), lambda b, t: (b, 0, t, 0))

    def head_out(n):
        return jax.ShapeDtypeStruct((batch, n, seq, HEAD_DIM), BF16)

    gate_out = jax.ShapeDtypeStruct((batch, seq, GDN_V_HEADS), F32)
    return pl.pallas_call(
        _gdn_in_kernel,
        grid=(batch, seq // tm),
        in_specs=[rows(d), _resident(wqkv.shape), _resident(wb.shape), _resident(wa.shape),
                  _resident(conv_w.shape), _resident(alog.shape), _resident(dtb.shape)],
        out_specs=[heads(GDN_QK_HEADS), heads(GDN_QK_HEADS), heads(GDN_V_HEADS),
                   rows(GDN_V_HEADS), rows(GDN_V_HEADS)],
        out_shape=[head_out(GDN_QK_HEADS), head_out(GDN_QK_HEADS), head_out(GDN_V_HEADS),
                   gate_out, gate_out],
        scratch_shapes=[pltpu.VMEM((tm + SUBLANES, GDN_CONV_DIM), F32)],
        compiler_params=_params(("parallel", "arbitrary")),
        name="gdn_in",
    )(h3, wqkv, wb, wa, conv_w, alog, dtb)


GDN_TILE = 256


def _gdn_core_kernel(q_ref, k_ref, v_ref, gcol_ref, bcol_ref, grow_ref, o_ref,
                     s_ref, u_ref, wq_ref, kd_ref, attn_ref, gl_ref, bd_ref):
    tile = GDN_TILE
    c_len = GDN_CHUNK
    n_chunks = tile // c_len
    group = GDN_V_HEADS // GDN_QK_HEADS

    @pl.when(pl.program_id(1) == 0)
    def _():
        s_ref[...] = jnp.zeros_like(s_ref)

    ii = lax.broadcasted_iota(jnp.int32, (tile, tile), 0)
    jj = lax.broadcasted_iota(jnp.int32, (tile, tile), 1)
    bd_ref[...] = ((ii // c_len) == (jj // c_len)).astype(BF16)
    ci = lax.broadcasted_iota(jnp.int32, (c_len, tile), 0)
    cj = lax.broadcasted_iota(jnp.int32, (c_len, tile), 1)
    cj_in = cj % c_len
    cj_chunk = cj // c_len
    cat_lower = ci >= cj_in
    cat_strict = ci > cj_in
    eye_cat = (ci == cj_in).astype(F32)
    half_lane = lax.broadcasted_iota(jnp.int32, (c_len, LANES), 1) < c_len
    head_lane = lax.broadcasted_iota(jnp.int32, (tile, GDN_V_HEADS), 1)

    def diag_blocks(x_full):
        out = x_full[0:c_len]
        for c in range(1, n_chunks):
            out = jnp.where(cj_chunk == c, x_full[c * c_len:(c + 1) * c_len], out)
        return out

    def rows_to_cat(x_b):
        halves = []
        for c in range(0, n_chunks, 2):
            halves.append(jnp.where(half_lane, x_b[c * c_len:(c + 1) * c_len],
                                    x_b[(c + 1) * c_len:(c + 2) * c_len]))
        return jnp.concatenate(halves, axis=1)

    def expand(x_cat):
        return jnp.concatenate([x_cat] * n_chunks, axis=0) * bd_ref[...]

    def prepare(hk, q_t, k_t, kk_cat, qk_cat, j):
        hv = hk * group + j
        pick = head_lane == hv
        gcc = jnp.sum(jnp.where(pick, gcol_ref[...], 0.0), axis=-1, keepdims=True)
        beta = jnp.sum(jnp.where(pick, bcol_ref[...], 0.0), axis=-1, keepdims=True)
        gcc_b = jnp.broadcast_to(gcc, (tile, LANES))
        beta_b = jnp.broadcast_to(beta, (tile, LANES))
        gcr = grow_ref[hv]
        decay = jnp.exp(jnp.where(cat_lower, rows_to_cat(gcc_b) - gcr, NEG_INF))
        attn_cat = (qk_cat * decay).astype(BF16)
        for c in range(n_chunks):
            attn_ref[hv, c] = attn_cat[:, c * c_len:(c + 1) * c_len]
        p_cat = jnp.where(cat_strict, -(rows_to_cat(beta_b) * kk_cat * decay), 0.0)
        gam_b = jnp.exp(gcc_b)
        k32 = k_t.astype(F32)
        v32 = v_ref[hv].astype(F32)
        rhs = jnp.concatenate([v32 * beta_b, k32 * (beta_b * gam_b)], axis=1).astype(BF16)
        q_dec = (q_t.astype(F32) * gam_b).astype(BF16)
        g_last_b = jnp.concatenate(
            [jnp.broadcast_to(gcc_b[(c + 1) * c_len - 1:(c + 1) * c_len], (c_len, LANES))
             for c in range(n_chunks)], axis=0)
        kd_ref[hv] = (k32 * jnp.exp(g_last_b - gcc_b)).astype(BF16)
        for c in range(n_chunks):
            wq_ref[hv, (2 * c + 1) * c_len:(2 * c + 2) * c_len] = q_dec[c * c_len:(c + 1) * c_len]
            gl_ref[hv, c] = jnp.exp(g_last_b[c * c_len:c * c_len + SUBLANES])
        return dict(hv=hv, p_bd=expand(p_cat.astype(BF16)), p_cat=p_cat, t_cat=eye_cat + p_cat, rhs=rhs)

    def head_program(hk, j, shared):
        if not shared:
            q_t = q_ref[hk]
            k_t = k_ref[hk]
            shared.update(q_t=q_t, k_t=k_t, kk_cat=diag_blocks(_dot_nt(k_t, k_t)),
                          qk_cat=diag_blocks(_dot_nt(q_t, k_t)))
        h = prepare(hk, shared["q_t"], shared["k_t"], shared["kk_cat"], shared["qk_cat"], j)
        yield
        p_cat = _dot(h["p_cat"].astype(BF16), h["p_bd"])
        t_cat = h["t_cat"]
        yield
        n = 2
        while 2 * n < c_len:
            both = _dot(jnp.concatenate([t_cat, p_cat], axis=0).astype(BF16), expand(p_cat.astype(BF16)))
            t_cat = t_cat + both[:c_len]
            p_cat = both[c_len:]
            n *= 2
            yield
        t_cat = t_cat + _dot(t_cat.astype(BF16), expand(p_cat.astype(BF16)))
        yield
        uw = _dot(expand(t_cat.astype(BF16)), h["rhs"])
        hv = h["hv"]
        u_ref[hv] = uw[:, :HEAD_DIM]
        w_b = uw[:, HEAD_DIM:].astype(BF16)
        for c in range(n_chunks):
            wq_ref[hv, 2 * c * c_len:(2 * c + 1) * c_len] = w_b[c * c_len:(c + 1) * c_len]

    progs = []
    for hk in range(GDN_QK_HEADS):
        shared = {}
        for j in range(group):
            progs.append(head_program(hk, j, shared))
    while progs:
        progs = [prog for prog in progs if next(prog, "done") != "done"]

    def phase2(c, carry):
        r0 = pl.multiple_of(c * c_len, c_len)
        r2 = pl.multiple_of(c * 2 * c_len, 2 * c_len)
        boths = [_dot(wq_ref[hv, pl.ds(r2, 2 * c_len), :], s_ref[hv].astype(BF16))
                 for hv in range(GDN_V_HEADS)]
        for hv in range(GDN_V_HEADS):
            v_new = (u_ref[hv, pl.ds(r0, c_len), :] - boths[hv][:c_len]).astype(BF16)
            o_ref[pl.ds(r0, c_len), hv * HEAD_DIM:(hv + 1) * HEAD_DIM] = (
                boths[hv][c_len:] + _dot(attn_ref[hv, c], v_new))
            s_ref[hv] = s_ref[hv] * gl_ref[hv, c][0:1, :] + lax.dot_general(
                kd_ref[hv, pl.ds(r0, c_len), :], v_new, TN_DIMS, preferred_element_type=F32)
        return carry

    lax.fori_loop(0, n_chunks, phase2, 0)


def _gdn_core(q, k, v, gc, beta):
    batch, _, seq, _ = q.shape
    tile = GDN_TILE
    n_chunks = tile // GDN_CHUNK
    nh = GDN_V_HEADS
    g_row = gc.transpose(0, 2, 1).reshape(batch, nh, seq // tile, 1, tile)

    def heads(n):
        return pl.BlockSpec((None, n, tile, HEAD_DIM), lambda b, t: (b, 0, t, 0))

    col_spec = pl.BlockSpec((None, tile, nh), lambda b, t: (b, t, 0))
    row_spec = pl.BlockSpec((None, nh, None, 1, tile), lambda b, t: (b, 0, t, 0, 0))
    return pl.pallas_call(
        _gdn_core_kernel,
        grid=(batch, seq // tile),
        in_specs=[heads(GDN_QK_HEADS), heads(GDN_QK_HEADS), heads(nh), col_spec, col_spec, row_spec],
        out_specs=pl.BlockSpec((None, tile, GDN_V_DIM), lambda b, t: (b, t, 0)),
        out_shape=jax.ShapeDtypeStruct((batch, seq, GDN_V_DIM), F32),
        scratch_shapes=[
            pltpu.VMEM((nh, HEAD_DIM, HEAD_DIM), F32),
            pltpu.VMEM((nh, tile, HEAD_DIM), F32),
            pltpu.VMEM((nh, 2 * tile, HEAD_DIM), BF16),
            pltpu.VMEM((nh, tile, HEAD_DIM), BF16),
            pltpu.VMEM((nh, n_chunks, GDN_CHUNK, GDN_CHUNK), BF16),
            pltpu.VMEM((nh, n_chunks, SUBLANES, LANES), F32),
            pltpu.VMEM((tile, tile), BF16),
        ],
        compiler_params=_params(("parallel", "arbitrary")),
        name="gdn_core",
    )(q, k, v, gc, beta, g_row)


GDN_OUT_ROWS = 256


def _gdn_out_kernel(h_ref, o_ref, wz_ref, nw_ref, wo_ref, g_ref, b_ref, out_ref, y_ref):
    h = h_ref[...]
    z = _dot(h.astype(BF16), wz_ref[...])
    nw = nw_ref[...]
    for n in range(GDN_V_HEADS):
        cs = slice(n * HEAD_DIM, (n + 1) * HEAD_DIM)
        o = o_ref[:, cs]
        zz = z[:, cs]
        y = o * lax.rsqrt(jnp.mean(o * o, axis=-1, keepdims=True) + RMS_EPS) * nw * (zz * _sigmoid(zz))
        y_ref[:, cs] = y.astype(BF16)
    y = ALPHA * h + _dot(y_ref[...], wo_ref[...])
    out_ref[...] = _layer_norm(y, g_ref[...], b_ref[...])


def _gdn_out(h, o, wz, nw, wo, g, b):
    m, d = h.shape
    tm = GDN_OUT_ROWS
    return pl.pallas_call(
        _gdn_out_kernel,
        grid=(m // tm,),
        in_specs=[pl.BlockSpec((tm, d), lambda i: (i, 0)),
                  pl.BlockSpec((tm, GDN_V_DIM), lambda i: (i, 0)),
                  _resident(wz.shape), _resident(nw.shape), _resident(wo.shape),
                  _resident(g.shape), _resident(b.shape)],
        out_specs=pl.BlockSpec((tm, d), lambda i: (i, 0)),
        out_shape=jax.ShapeDtypeStruct((m, d), F32),
        scratch_shapes=[pltpu.VMEM((tm, GDN_V_DIM), BF16)],
        compiler_params=_params(("parallel",)),
        name="gdn_out",
    )(h, o, wz, nw, wo, g, b)


def _row(vec):
    return vec.reshape(1, -1).astype(F32)


def _pad_lanes(x2d):
    return jnp.pad(x2d, ((0, 0), (0, LANES - x2d.shape[1])))


def kernel(x, ln_g, ln_b, ffn_pre_w_in, ffn_pre_w_out, ffn_post_w_in, ffn_post_w_out,
           moba_w_in, moba_w_out, gdn_w_in, gdn_conv_w, gdn_a_log, gdn_dt_bias,
           gdn_norm_w, gdn_w_out):
    batch, seq, d = x.shape
    m = batch * seq
    h = x.reshape(m, d).astype(F32)
    cos, sin = _rope_tables(seq)

    for i in range(DEPTH):
        h = _ffn_block(h, ffn_pre_w_in[i].astype(BF16), ffn_pre_w_out[i].astype(BF16),
                       _row(ln_g[i, 0]), _row(ln_b[i, 0]))
        j = i // 2
        if i % 2 == 0:
            w_in = moba_w_in[j]
            q, k, vt = _qkv_rope(h.reshape(batch, seq, d), w_in[:, :2 * D_MODEL].astype(BF16),
                                 w_in[:, 2 * D_MODEL:].T.astype(BF16), cos, sin)
            o = _moba_attention(q, k, vt)
            h = _proj_ln(h, o.reshape(m, d), moba_w_out[j].astype(BF16),
                         _row(ln_g[i, 1]), _row(ln_b[i, 1]))
        else:
            w_in = gdn_w_in[j]
            z0 = GDN_CONV_DIM
            b0 = z0 + GDN_V_DIM
            a0 = b0 + GDN_V_HEADS
            wqkv = w_in[:, :z0].astype(BF16)
            wz = w_in[:, z0:b0].astype(BF16)
            wb = _pad_lanes(w_in[:, b0:a0]).astype(BF16)
            wa = _pad_lanes(w_in[:, a0:a0 + GDN_V_HEADS]).astype(BF16)
            q, k, v, beta, gc = _gdn_in(
                h.reshape(batch, seq, d), wqkv, wb, wa, gdn_conv_w[j].astype(F32),
                _pad_lanes(_row(gdn_a_log[j])), _pad_lanes(_row(gdn_dt_bias[j])))
            o = _gdn_core(q, k, v, gc, beta)
            h = _gdn_out(h, o.reshape(m, GDN_V_DIM), wz, _row(gdn_norm_w[j]),
                         gdn_w_out[j].astype(BF16), _row(ln_g[i, 1]), _row(ln_b[i, 1]))
        h = _ffn_block(h, ffn_post_w_in[i].astype(BF16), ffn_post_w_out[i].astype(BF16),
                       _row(ln_g[i, 2]), _row(ln_b[i, 2]))
    return h.reshape(batch, seq, d).astype(x.dtype)
```

```python
import functools
import math

import jax
import jax.numpy as jnp
from jax import lax
from jax.experimental import pallas as pl
from jax.experimental.pallas import tpu as pltpu

F32 = jnp.float32
BF16 = jnp.bfloat16

D_MODEL = 1024
DEPTH = 2
D_FF = 2816
HEAD_DIM = 128
MOBA_HEADS = D_MODEL // HEAD_DIM
MOBA_BLOCK = 256
MOBA_TOPK = 3
ROPE_THETA = 10000.0
GDN_QK_HEADS = D_MODEL // HEAD_DIM
GDN_V_HEADS = 2 * GDN_QK_HEADS
GDN_QK_DIM = GDN_QK_HEADS * HEAD_DIM
GDN_V_DIM = GDN_V_HEADS * HEAD_DIM
GDN_CONV_DIM = 2 * GDN_QK_DIM + GDN_V_DIM
GDN_CONV = 4
GDN_CHUNK = 64
ALPHA = (2 * DEPTH) ** 0.25
HALF_STEP = 0.5
LN_EPS = 1e-5
RMS_EPS = 1e-6
NEG_INF = -1e30

LANES = 128
SUBLANES = 8
VMEM_LIMIT = 56 * 1024 * 1024

NT_DIMS = (((1,), (1,)), ((), ()))
TN_DIMS = (((0,), (0,)), ((), ()))


def _dot(a, b):
    return jnp.dot(a, b, preferred_element_type=F32)


def _dot_nt(a, b):
    return lax.dot_general(a, b, NT_DIMS, preferred_element_type=F32)


def _sigmoid(x):
    return 1.0 / (1.0 + jnp.exp(-x))


def _layer_norm(y, g, b):
    mu = jnp.mean(y, axis=-1, keepdims=True)
    d = y - mu
    var = jnp.mean(d * d, axis=-1, keepdims=True)
    return d * lax.rsqrt(var + LN_EPS) * g + b


def _resident(shape):
    nd = len(shape)
    return pl.BlockSpec(shape, lambda *_: (0,) * nd, pipeline_mode=pl.Buffered(1))


def _params(semantics):
    return pltpu.CompilerParams(dimension_semantics=semantics, vmem_limit_bytes=VMEM_LIMIT)


FFN_ROWS = 512


def _ffn_kernel(h_ref, win_ref, wout_ref, g_ref, b_ref, o_ref):
    h = h_ref[...]
    hb = h.astype(BF16)
    gate = _dot(hb, win_ref[:, :D_FF])
    up = _dot(hb, win_ref[:, D_FF:])
    act = (gate * _sigmoid(gate) * up).astype(BF16)
    y = ALPHA * h + HALF_STEP * _dot(act, wout_ref[...])
    o_ref[...] = _layer_norm(y, g_ref[...], b_ref[...])


def _ffn_block(h, w_in, w_out, g, b):
    m, d = h.shape
    return pl.pallas_call(
        _ffn_kernel,
        grid=(m // FFN_ROWS,),
        in_specs=[
            pl.BlockSpec((FFN_ROWS, d), lambda i: (i, 0)),
            _resident(w_in.shape),
            _resident(w_out.shape),
            _resident(g.shape),
            _resident(b.shape),
        ],
        out_specs=pl.BlockSpec((FFN_ROWS, d), lambda i: (i, 0)),
        out_shape=jax.ShapeDtypeStruct((m, d), F32),
        compiler_params=_params(("parallel",)),
        name="ffn_ln",
    )(h, w_in, w_out, g, b)


PROJ_ROWS = 512


def _proj_ln_kernel(h_ref, x_ref, w_ref, g_ref, b_ref, o_ref):
    y = ALPHA * h_ref[...] + _dot(x_ref[...], w_ref[...])
    o_ref[...] = _layer_norm(y, g_ref[...], b_ref[...])


def _proj_ln(h, x, w, g, b):
    m, d = h.shape
    return pl.pallas_call(
        _proj_ln_kernel,
        grid=(m // PROJ_ROWS,),
        in_specs=[
            pl.BlockSpec((PROJ_ROWS, d), lambda i: (i, 0)),
            pl.BlockSpec((PROJ_ROWS, x.shape[1]), lambda i: (i, 0)),
            _resident(w.shape),
            _resident(g.shape),
            _resident(b.shape),
        ],
        out_specs=pl.BlockSpec((PROJ_ROWS, d), lambda i: (i, 0)),
        out_shape=jax.ShapeDtypeStruct((m, d), F32),
        compiler_params=_params(("parallel",)),
        name="proj_ln",
    )(h, x, w, g, b)


QKV_ROWS = 512
BF16_SUBLANES = 16
MOBA_VT_ROWS = HEAD_DIM + BF16_SUBLANES
MOBA_EXP2_SCALE = (HEAD_DIM ** -0.5) * math.log2(math.e)


def _qkv_rope_kernel(h_ref, wqk_ref, wvt_ref, cos_ref, sin_ref, q_ref, k_ref, vt_ref):
    hb = h_ref[...].astype(BF16)
    cos = cos_ref[...]
    sin = sin_ref[...]
    for part, out_ref, gain in ((0, q_ref, MOBA_EXP2_SCALE), (1, k_ref, 1.0)):
        x = _dot(hb, wqk_ref[:, part * D_MODEL:(part + 1) * D_MODEL])
        for hh in range(MOBA_HEADS):
            xs = x[:, hh * HEAD_DIM:(hh + 1) * HEAD_DIM]
            rot = pltpu.roll(xs, HEAD_DIM // 2, axis=1)
            y = xs * cos + rot * sin
            if gain != 1.0:
                y = y * gain
            out_ref[:, hh * HEAD_DIM:(hh + 1) * HEAD_DIM] = y.astype(BF16)
    vt = _dot_nt(wvt_ref[...], hb).astype(BF16)
    ones = jnp.ones((MOBA_VT_ROWS - HEAD_DIM, MOBA_BLOCK), BF16)
    for hh in range(MOBA_HEADS):
        for blk in range(QKV_ROWS // MOBA_BLOCK):
            vt_ref[hh, blk, 0:HEAD_DIM, :] = vt[hh * HEAD_DIM:(hh + 1) * HEAD_DIM,
                                                blk * MOBA_BLOCK:(blk + 1) * MOBA_BLOCK]
            vt_ref[hh, blk, HEAD_DIM:MOBA_VT_ROWS, :] = ones


def _rope_tables(seq):
    half = HEAD_DIM // 2
    inv_freq = ROPE_THETA ** (-jnp.arange(half, dtype=F32) / half)
    ang = jnp.arange(seq).astype(F32)[:, None] * inv_freq[None, :]
    cos = jnp.cos(ang)
    sin = jnp.sin(ang)
    return jnp.concatenate([cos, cos], axis=-1), jnp.concatenate([-sin, sin], axis=-1)


def _qkv_rope(h3, wqk, wvt, cos, sin):
    batch, seq, d = h3.shape
    tm = QKV_ROWS
    blocks = tm // MOBA_BLOCK
    row_spec = pl.BlockSpec((None, tm, d), lambda b, t: (b, t, 0))
    tab_spec = pl.BlockSpec((tm, HEAD_DIM), lambda b, t: (t, 0))
    vt_spec = pl.BlockSpec((None, MOBA_HEADS, blocks, MOBA_VT_ROWS, MOBA_BLOCK), lambda b, t: (b, 0, t, 0, 0))
    qk_out = jax.ShapeDtypeStruct((batch, seq, d), BF16)
    vt_out = jax.ShapeDtypeStruct((batch, MOBA_HEADS, seq // MOBA_BLOCK, MOBA_VT_ROWS, MOBA_BLOCK), BF16)
    return pl.pallas_call(
        _qkv_rope_kernel,
        grid=(batch, seq // tm),
        in_specs=[row_spec, _resident(wqk.shape), _resident(wvt.shape), tab_spec, tab_spec],
        out_specs=[row_spec, row_spec, vt_spec],
        out_shape=[qk_out, qk_out, vt_out],
        compiler_params=_params(("parallel", "parallel")),
        name="moba_qkv_rope",
    )(h3, wqk, wvt, cos, sin)


MOBA_TILES_PER_STEP = 8


def _moba_attn_kernel(q_ref, k_ref, vt_ref, o_ref, kmean_ref, *, n_blk, n_pad, per_step):
    step = pl.program_id(2)
    bs = MOBA_BLOCK

    @pl.when(step == 0)
    def _():
        kmean_ref[...] = jnp.zeros_like(kmean_ref)
        for n in range(n_blk):
            kb = k_ref[n * bs:(n + 1) * bs, :].astype(F32)
            kmean_ref[n:n + 1, :] = jnp.sum(kb, axis=0, keepdims=True) * (1.0 / bs)

    def tile(n_past):
        q = q_ref[n_past * bs:(n_past + 1) * bs, :]
        if n_past > 0:
            gate = _dot_nt(kmean_ref[...].astype(BF16), q)
        s = _dot_nt(k_ref[n_past * bs:(n_past + 1) * bs, :], q)
        if n_past > 0:
            s_past = _dot_nt(k_ref[0:n_past * bs, :], q)
        yield

        key = lax.broadcasted_iota(jnp.int32, s.shape, 0)
        qry = lax.broadcasted_iota(jnp.int32, s.shape, 1)
        s = jnp.where(key <= qry, s, NEG_INF)
        m = jnp.max(s, axis=0, keepdims=True)
        p = jnp.exp2(s - m)
        acc = _dot(vt_ref[n_past], p.astype(BF16))
        yield

        if n_past > 0:
            blk_f = lax.broadcasted_iota(jnp.int32, gate.shape, 0).astype(F32)
            past = blk_f < float(n_past)
            gate = jnp.where(past, gate, NEG_INF)
            sel = jnp.zeros(gate.shape, F32)
            for _ in range(min(MOBA_TOPK, n_blk)):
                top = jnp.max(gate, axis=0, keepdims=True)
                first = jnp.min(jnp.where(gate == top, blk_f, float(n_pad)), axis=0, keepdims=True)
                pick = blk_f == first
                sel = jnp.where(jnp.logical_and(pick, past), 1.0, sel)
                gate = jnp.where(pick, -jnp.inf, gate)

            for n in range(n_past):
                chosen = sel[n:n + 1, :] > 0.5
                s = s_past[n * bs:(n + 1) * bs]
                m_blk = jnp.where(chosen, jnp.max(s, axis=0, keepdims=True), NEG_INF)
                m_new = jnp.maximum(m, m_blk)
                corr = jnp.exp2(m - m_new)
                p = jnp.exp2(s - jnp.where(chosen, m_new, jnp.inf))
                acc = acc * corr + _dot(vt_ref[n], p.astype(BF16))
                m = m_new
                yield

        o = acc[0:HEAD_DIM] / acc[HEAD_DIM:HEAD_DIM + 1]
        o_ref[n_past * bs:(n_past + 1) * bs, :] = o.T.astype(o_ref.dtype)

    def tile_group(i):
        half = per_step // 2
        per_half = n_blk // per_step
        tiles = []
        for g in range(half):
            lo = g * 2 * per_half + i
            tiles += [n_blk - 1 - lo, lo]
        progs = [tile(n) for n in sorted(tiles, reverse=True)]
        while progs:
            progs = [prog for prog in progs if next(prog, "done") != "done"]

    for i in range(n_blk // per_step):
        pl.when(step == i)(functools.partial(tile_group, i))


def _moba_attention(q, k, vt):
    batch, seq, _ = q.shape
    n_blk = seq // MOBA_BLOCK
    n_pad = -(-n_blk // SUBLANES) * SUBLANES
    per_step = min(MOBA_TILES_PER_STEP, n_blk)
    assert seq % MOBA_BLOCK == 0 and per_step % 2 == 0 and n_blk % per_step == 0
    q_spec = pl.BlockSpec((None, seq, HEAD_DIM), lambda b, h, i: (b, 0, h))
    k_spec = q_spec
    vt_spec = pl.BlockSpec((None, None, n_blk, MOBA_VT_ROWS, MOBA_BLOCK), lambda b, h, i: (b, h, 0, 0, 0))
    return pl.pallas_call(
        functools.partial(_moba_attn_kernel, n_blk=n_blk, n_pad=n_pad, per_step=per_step),
        grid=(batch, MOBA_HEADS, n_blk // per_step),
        in_specs=[q_spec, k_spec, vt_spec],
        out_specs=q_spec,
        out_shape=jax.ShapeDtypeStruct(q.shape, BF16),
        scratch_shapes=[
            pltpu.VMEM((n_pad, HEAD_DIM), F32),
        ],
        compiler_params=_params(("parallel", "parallel", "arbitrary")),
        name="moba_attn",
    )(q, k, vt)


GDN_IN_ROWS = 256


def _gdn_in_kernel(h_ref, wqkv_ref, wb_ref, wa_ref, cw_ref, alog_ref, dtb_ref,
                   q_ref, k_ref, v_ref, beta_ref, gc_ref, xs_ref):
    tm = GDN_IN_ROWS
    halo = SUBLANES
    hb = h_ref[...].astype(BF16)

    @pl.when(pl.program_id(1) == 0)
    def _():
        xs_ref[0:halo, :] = jnp.zeros((halo, GDN_CONV_DIM), F32)

    xs_ref[halo:halo + tm, :] = _dot(hb, wqkv_ref[...])

    n_cols = GDN_CONV_DIM // HEAD_DIM
    for c in range(n_cols):
        cs = slice(c * HEAD_DIM, (c + 1) * HEAD_DIM)
        xe = xs_ref[:, cs]
        y = xe[halo:] * cw_ref[GDN_CONV - 1:GDN_CONV, cs]
        for back in range(1, GDN_CONV):
            i = GDN_CONV - 1 - back
            y = y + pltpu.roll(xe, back, axis=0)[halo:] * cw_ref[i:i + 1, cs]
        y = y * _sigmoid(y)
        if c < 2 * GDN_QK_HEADS:
            y = y * lax.rsqrt(jnp.sum(y * y, axis=-1, keepdims=True) + RMS_EPS)
            if c < GDN_QK_HEADS:
                q_ref[c] = (y * (HEAD_DIM ** -0.5)).astype(BF16)
            else:
                k_ref[c - GDN_QK_HEADS] = y.astype(BF16)
        else:
            v_ref[c - 2 * GDN_QK_HEADS] = y.astype(BF16)

    xs_ref[0:halo, :] = xs_ref[tm:tm + halo, :]

    beta = _sigmoid(_dot(hb, wb_ref[...]))
    a = _dot(hb, wa_ref[...]) + dtb_ref[...]
    softplus = jnp.maximum(a, 0.0) + jnp.log1p(jnp.exp(-jnp.abs(a)))
    g = -jnp.exp(alog_ref[...]) * softplus
    ri = lax.broadcasted_iota(jnp.int32, (tm, tm), 0)
    ci = lax.broadcasted_iota(jnp.int32, (tm, tm), 1)
    tri = jnp.logical_and(ri >= ci, ri // GDN_CHUNK == ci // GDN_CHUNK).astype(F32)
    gc = jnp.dot(tri, g, preferred_element_type=F32, precision=lax.Precision.HIGHEST)
    beta_ref[...] = beta[:, :GDN_V_HEADS]
    gc_ref[...] = gc[:, :GDN_V_HEADS]


def _gdn_in(h3, wqkv, wb, wa, conv_w, alog, dtb):
    batch, seq, d = h3.shape
    tm = GDN_IN_ROWS

    def rows(width):
        return pl.BlockSpec((None, tm, width), lambda b, t: (b, t, 0))

    def heads(n):
        return pl.BlockSpec((None, n, tm, HEAD_DIM), lambda b, t: (b, 0, t, 0))

    def head_out(n):
        return jax.ShapeDtypeStruct((batch, n, seq, HEAD_DIM), BF16)

    gate_out = jax.ShapeDtypeStruct((batch, seq, GDN_V_HEADS), F32)
    return pl.pallas_call(
        _gdn_in_kernel,
        grid=(batch, seq // tm),
        in_specs=[rows(d), _resident(wqkv.shape), _resident(wb.shape), _resident(wa.shape),
                  _resident(conv_w.shape), _resident(alog.shape), _resident(dtb.shape)],
        out_specs=[heads(GDN_QK_HEADS), heads(GDN_QK_HEADS), heads(GDN_V_HEADS),
                   rows(GDN_V_HEADS), rows(GDN_V_HEADS)],
        out_shape=[head_out(GDN_QK_HEADS), head_out(GDN_QK_HEADS), head_out(GDN_V_HEADS),
                   gate_out, gate_out],
        scratch_shapes=[pltpu.VMEM((tm + SUBLANES, GDN_CONV_DIM), F32)],
        compiler_params=_params(("parallel", "arbitrary")),
        name="gdn_in",
    )(h3, wqkv, wb, wa, conv_w, alog, dtb)


GDN_TILE = 256


def _gdn_core_kernel(q_ref, k_ref, v_ref, gcol_ref, bcol_ref, grow_ref, o_ref,
                     s_ref, u_ref, wq_ref, kd_ref, attn_ref, gl_ref, bd_ref):
    tile = GDN_TILE
    c_len = GDN_CHUNK
    n_chunks = tile // c_len
    group = GDN_V_HEADS // GDN_QK_HEADS

    @pl.when(pl.program_id(1) == 0)
    def _():
        s_ref[...] = jnp.zeros_like(s_ref)

    ii = lax.broadcasted_iota(jnp.int32, (tile, tile), 0)
    jj = lax.broadcasted_iota(jnp.int32, (tile, tile), 1)
    bd_ref[...] = ((ii // c_len) == (jj // c_len)).astype(BF16)
    ci = lax.broadcasted_iota(jnp.int32, (c_len, tile), 0)
    cj = lax.broadcasted_iota(jnp.int32, (c_len, tile), 1)
    cj_in = cj % c_len
    cj_chunk = cj // c_len
    cat_lower = ci >= cj_in
    cat_strict = ci > cj_in
    eye_cat = (ci == cj_in).astype(F32)
    half_lane = lax.broadcasted_iota(jnp.int32, (c_len, LANES), 1) < c_len
    head_lane = lax.broadcasted_iota(jnp.int32, (tile, GDN_V_HEADS), 1)

    def diag_blocks(x_full):
        out = x_full[0:c_len]
        for c in range(1, n_chunks):
            out = jnp.where(cj_chunk == c, x_full[c * c_len:(c + 1) * c_len], out)
        return out

    def rows_to_cat(x_b):
        halves = []
        for c in range(0, n_chunks, 2):
            halves.append(jnp.where(half_lane, x_b[c * c_len:(c + 1) * c_len],
                                    x_b[(c + 1) * c_len:(c + 2) * c_len]))
        return jnp.concatenate(halves, axis=1)

    def expand(x_cat):
        return jnp.concatenate([x_cat] * n_chunks, axis=0) * bd_ref[...]

    def prepare(hk, q_t, k_t, kk_cat, qk_cat, j):
        hv = hk * group + j
        pick = head_lane == hv
        gcc = jnp.sum(jnp.where(pick, gcol_ref[...], 0.0), axis=-1, keepdims=True)
        beta = jnp.sum(jnp.where(pick, bcol_ref[...], 0.0), axis=-1, keepdims=True)
        gcc_b = jnp.broadcast_to(gcc, (tile, LANES))
        beta_b = jnp.broadcast_to(beta, (tile, LANES))
        gcr = grow_ref[hv]
        decay = jnp.exp(jnp.where(cat_lower, rows_to_cat(gcc_b) - gcr, NEG_INF))
        attn_cat = (qk_cat * decay).astype(BF16)
        for c in range(n_chunks):
            attn_ref[hv, c] = attn_cat[:, c * c_len:(c + 1) * c_len]
        p_cat = jnp.where(cat_strict, -(rows_to_cat(beta_b) * kk_cat * decay), 0.0)
        gam_b = jnp.exp(gcc_b)
        k32 = k_t.astype(F32)
        v32 = v_ref[hv].astype(F32)
        rhs = jnp.concatenate([v32 * beta_b, k32 * (beta_b * gam_b)], axis=1).astype(BF16)
        q_dec = (q_t.astype(F32) * gam_b).astype(BF16)
        g_last_b = jnp.concatenate(
            [jnp.broadcast_to(gcc_b[(c + 1) * c_len - 1:(c + 1) * c_len], (c_len, LANES))
             for c in range(n_chunks)], axis=0)
        kd_ref[hv] = (k32 * jnp.exp(g_last_b - gcc_b)).astype(BF16)
        for c in range(n_chunks):
            wq_ref[hv, (2 * c + 1) * c_len:(2 * c + 2) * c_len] = q_dec[c * c_len:(c + 1) * c_len]
            gl_ref[hv, c] = jnp.exp(g_last_b[c * c_len:c * c_len + SUBLANES])
        return dict(hv=hv, p_bd=expand(p_cat.astype(BF16)), p_cat=p_cat, t_cat=eye_cat + p_cat, rhs=rhs)

    def head_program(hk, j, shared):
        if not shared:
            q_t = q_ref[hk]
            k_t = k_ref[hk]
            shared.update(q_t=q_t, k_t=k_t, kk_cat=diag_blocks(_dot_nt(k_t, k_t)),
                          qk_cat=diag_blocks(_dot_nt(q_t, k_t)))
        h = prepare(hk, shared["q_t"], shared["k_t"], shared["kk_cat"], shared["qk_cat"], j)
        yield
        p_cat = _dot(h["p_cat"].astype(BF16), h["p_bd"])
        t_cat = h["t_cat"]
        yield
        n = 2
        while 2 * n < c_len:
            both = _dot(jnp.concatenate([t_cat, p_cat], axis=0).astype(BF16), expand(p_cat.astype(BF16)))
            t_cat = t_cat + both[:c_len]
            p_cat = both[c_len:]
            n *= 2
            yield
        t_cat = t_cat + _dot(t_cat.astype(BF16), expand(p_cat.astype(BF16)))
        yield
        uw = _dot(expand(t_cat.astype(BF16)), h["rhs"])
        hv = h["hv"]
        u_ref[hv] = uw[:, :HEAD_DIM]
        w_b = uw[:, HEAD_DIM:].astype(BF16)
        for c in range(n_chunks):
            wq_ref[hv, 2 * c * c_len:(2 * c + 1) * c_len] = w_b[c * c_len:(c + 1) * c_len]

    progs = []
    for hk in range(GDN_QK_HEADS):
        shared = {}
        for j in range(group):
            progs.append(head_program(hk, j, shared))
    while progs:
        progs = [prog for prog in progs if next(prog, "done") != "done"]

    def phase2(c, carry):
        r0 = pl.multiple_of(c * c_len, c_len)
        r2 = pl.multiple_of(c * 2 * c_len, 2 * c_len)
        boths = [_dot(wq_ref[hv, pl.ds(r2, 2 * c_len), :], s_ref[hv].astype(BF16))
                 for hv in range(GDN_V_HEADS)]
        for hv in range(GDN_V_HEADS):
            v_new = (u_ref[hv, pl.ds(r0, c_len), :] - boths[hv][:c_len]).astype(BF16)
            o_ref[pl.ds(r0, c_len), hv * HEAD_DIM:(hv + 1) * HEAD_DIM] = (
                boths[hv][c_len:] + _dot(attn_ref[hv, c], v_new))
            s_ref[hv] = s_ref[hv] * gl_ref[hv, c][0:1, :] + lax.dot_general(
                kd_ref[hv, pl.ds(r0, c_len), :], v_new, TN_DIMS, preferred_element_type=F32)
        return carry

    lax.fori_loop(0, n_chunks, phase2, 0)


def _gdn_core(q, k, v, gc, beta):
    batch, _, seq, _ = q.shape
    tile = GDN_TILE
    n_chunks = tile // GDN_CHUNK
    nh = GDN_V_HEADS
    g_row = gc.transpose(0, 2, 1).reshape(batch, nh, seq // tile, 1, tile)

    def heads(n):
        return pl.BlockSpec((None, n, tile, HEAD_DIM), lambda b, t: (b, 0, t, 0))

    col_spec = pl.BlockSpec((None, tile, nh), lambda b, t: (b, t, 0))
    row_spec = pl.BlockSpec((None, nh, None, 1, tile), lambda b, t: (b, 0, t, 0, 0))
    return pl.pallas_call(
        _gdn_core_kernel,
        grid=(batch, seq // tile),
        in_specs=[heads(GDN_QK_HEADS), heads(GDN_QK_HEADS), heads(nh), col_spec, col_spec, row_spec],
        out_specs=pl.BlockSpec((None, tile, GDN_V_DIM), lambda b, t: (b, t, 0)),
        out_shape=jax.ShapeDtypeStruct((batch, seq, GDN_V_DIM), F32),
        scratch_shapes=[
            pltpu.VMEM((nh, HEAD_DIM, HEAD_DIM), F32),
            pltpu.VMEM((nh, tile, HEAD_DIM), F32),
            pltpu.VMEM((nh, 2 * tile, HEAD_DIM), BF16),
            pltpu.VMEM((nh, tile, HEAD_DIM), BF16),
            pltpu.VMEM((nh, n_chunks, GDN_CHUNK, GDN_CHUNK), BF16),
            pltpu.VMEM((nh, n_chunks, SUBLANES, LANES), F32),
            pltpu.VMEM((tile, tile), BF16),
        ],
        compiler_params=_params(("parallel", "arbitrary")),
        name="gdn_core",
    )(q, k, v, gc, beta, g_row)


GDN_OUT_ROWS = 256


def _gdn_out_kernel(h_ref, o_ref, wz_ref, nw_ref, wo_ref, g_ref, b_ref, out_ref, y_ref):
    h = h_ref[...]
    z = _dot(h.astype(BF16), wz_ref[...])
    nw = nw_ref[...]
    for n in range(GDN_V_HEADS):
        cs = slice(n * HEAD_DIM, (n + 1) * HEAD_DIM)
        o = o_ref[:, cs]
        zz = z[:, cs]
        y = o * lax.rsqrt(jnp.mean(o * o, axis=-1, keepdims=True) + RMS_EPS) * nw * (zz * _sigmoid(zz))
        y_ref[:, cs] = y.astype(BF16)
    y = ALPHA * h + _dot(y_ref[...], wo_ref[...])
    out_ref[...] = _layer_norm(y, g_ref[...], b_ref[...])


def _gdn_out(h, o, wz, nw, wo, g, b):
    m, d = h.shape
    tm = GDN_OUT_ROWS
    return pl.pallas_call(
        _gdn_out_kernel,
        grid=(m // tm,),
        in_specs=[pl.BlockSpec((tm, d), lambda i: (i, 0)),
                  pl.BlockSpec((tm, GDN_V_DIM), lambda i: (i, 0)),
                  _resident(wz.shape), _resident(nw.shape), _resident(wo.shape),
                  _resident(g.shape), _resident(b.shape)],
        out_specs=pl.BlockSpec((tm, d), lambda i: (i, 0)),
        out_shape=jax.ShapeDtypeStruct((m, d), F32),
        scratch_shapes=[pltpu.VMEM((tm, GDN_V_DIM), BF16)],
        compiler_params=_params(("parallel",)),
        name="gdn_out",
    )(h, o, wz, nw, wo, g, b)


def _row(vec):
    return vec.reshape(1, -1).astype(F32)


def _pad_lanes(x2d):
    return jnp.pad(x2d, ((0, 0), (0, LANES - x2d.shape[1])))


def kernel(x, ln_g, ln_b, ffn_pre_w_in, ffn_pre_w_out, ffn_post_w_in, ffn_post_w_out,
           moba_w_in, moba_w_out, gdn_w_in, gdn_conv_w, gdn_a_log, gdn_dt_bias,
           gdn_norm_w, gdn_w_out):
    batch, seq, d = x.shape
    m = batch * seq
    h = x.reshape(m, d).astype(F32)
    cos, sin = _rope_tables(seq)

    for i in range(DEPTH):
        h = _ffn_block(h, ffn_pre_w_in[i].astype(BF16), ffn_pre_w_out[i].astype(BF16),
                       _row(ln_g[i, 0]), _row(ln_b[i, 0]))
        j = i // 2
        if i % 2 == 0:
            w_in = moba_w_in[j]
            q, k, vt = _qkv_rope(h.reshape(batch, seq, d), w_in[:, :2 * D_MODEL].astype(BF16),
                                 w_in[:, 2 * D_MODEL:].T.astype(BF16), cos, sin)
            o = _moba_attention(q, k, vt)
            h = _proj_ln(h, o.reshape(m, d), moba_w_out[j].astype(BF16),
                         _row(ln_g[i, 1]), _row(ln_b[i, 1]))
        else:
            w_in = gdn_w_in[j]
            z0 = GDN_CONV_DIM
            b0 = z0 + GDN_V_DIM
            a0 = b0 + GDN_V_HEADS
            wqkv = w_in[:, :z0].astype(BF16)
            wz = w_in[:, z0:b0].astype(BF16)
            wb = _pad_lanes(w_in[:, b0:a0]).astype(BF16)
            wa = _pad_lanes(w_in[:, a0:a0 + GDN_V_HEADS]).astype(BF16)
            q, k, v, beta, gc = _gdn_in(
                h.reshape(batch, seq, d), wqkv, wb, wa, gdn_conv_w[j].astype(F32),
                _pad_lanes(_row(gdn_a_log[j])), _pad_lanes(_row(gdn_dt_bias[j])))
            o = _gdn_core(q, k, v, gc, beta)
            h = _gdn_out(h, o.reshape(m, GDN_V_DIM), wz, _row(gdn_norm_w[j]),
                         gdn_w_out[j].astype(BF16), _row(ln_g[i, 1]), _row(ln_b[i, 1]))
        h = _ffn_block(h, ffn_post_w_in[i].astype(BF16), ffn_post_w_out[i].astype(BF16),
                       _row(ln_g[i, 2]), _row(ln_b[i, 2]))
    return h.reshape(batch, seq, d).astype(x.dtype)
```

```python
import functools
import math

import jax
import jax.numpy as jnp
from jax import lax
from jax.experimental import pallas as pl
from jax.experimental.pallas import tpu as pltpu

F32 = jnp.float32
BF16 = jnp.bfloat16

D_MODEL = 1024
DEPTH = 2
D_FF = 2816
HEAD_DIM = 128
MOBA_HEADS = D_MODEL // HEAD_DIM
MOBA_BLOCK = 256
MOBA_TOPK = 3
ROPE_THETA = 10000.0
GDN_QK_HEADS = D_MODEL // HEAD_DIM
GDN_V_HEADS = 2 * GDN_QK_HEADS
GDN_QK_DIM = GDN_QK_HEADS * HEAD_DIM
GDN_V_DIM = GDN_V_HEADS * HEAD_DIM
GDN_CONV_DIM = 2 * GDN_QK_DIM + GDN_V_DIM
GDN_CONV = 4
GDN_CHUNK = 64
ALPHA = (2 * DEPTH) ** 0.25
HALF_STEP = 0.5
LN_EPS = 1e-5
RMS_EPS = 1e-6
NEG_INF = -1e30

LANES = 128
SUBLANES = 8
VMEM_LIMIT = 56 * 1024 * 1024

NT_DIMS = (((1,), (1,)), ((), ()))
TN_DIMS = (((0,), (0,)), ((), ()))


def _dot(a, b):
    return jnp.dot(a, b, preferred_element_type=F32)


def _dot_nt(a, b):
    return lax.dot_general(a, b, NT_DIMS, preferred_element_type=F32)


def _sigmoid(x):
    return 1.0 / (1.0 + jnp.exp(-x))


def _layer_norm(y, g, b):
    mu = jnp.mean(y, axis=-1, keepdims=True)
    d = y - mu
    var = jnp.mean(d * d, axis=-1, keepdims=True)
    return d * lax.rsqrt(var + LN_EPS) * g + b


def _resident(shape):
    nd = len(shape)
    return pl.BlockSpec(shape, lambda *_: (0,) * nd, pipeline_mode=pl.Buffered(1))


def _params(semantics):
    return pltpu.CompilerParams(dimension_semantics=semantics, vmem_limit_bytes=VMEM_LIMIT)


FFN_ROWS = 1024
FFN_SUBTILES = 4


def _ffn_kernel(h_ref, win_ref, wout_ref, g_ref, b_ref, o_ref):
    sub = FFN_ROWS // FFN_SUBTILES
    for r in range(FFN_SUBTILES):
        rows = slice(r * sub, (r + 1) * sub)
        h = h_ref[rows, :]
        hb = h.astype(BF16)
        gate = _dot(hb, win_ref[:, :D_FF])
        up = _dot(hb, win_ref[:, D_FF:])
        act = (gate * _sigmoid(gate) * up).astype(BF16)
        y = ALPHA * h + HALF_STEP * _dot(act, wout_ref[...])
        o_ref[rows, :] = _layer_norm(y, g_ref[...], b_ref[...])


def _ffn_block(h, w_in, w_out, g, b):
    m, d = h.shape
    return pl.pallas_call(
        _ffn_kernel,
        grid=(m // FFN_ROWS,),
        in_specs=[
            pl.BlockSpec((FFN_ROWS, d), lambda i: (i, 0)),
            _resident(w_in.shape),
            _resident(w_out.shape),
            _resident(g.shape),
            _resident(b.shape),
        ],
        out_specs=pl.BlockSpec((FFN_ROWS, d), lambda i: (i, 0)),
        out_shape=jax.ShapeDtypeStruct((m, d), F32),
        compiler_params=_params(("parallel",)),
        name="ffn_ln",
    )(h, w_in, w_out, g, b)


PROJ_ROWS = 512


def _proj_ln_kernel(h_ref, x_ref, w_ref, g_ref, b_ref, o_ref):
    y = ALPHA * h_ref[...] + _dot(x_ref[...], w_ref[...])
    o_ref[...] = _layer_norm(y, g_ref[...], b_ref[...])


def _proj_ln(h, x, w, g, b):
    m, d = h.shape
    return pl.pallas_call(
        _proj_ln_kernel,
        grid=(m // PROJ_ROWS,),
        in_specs=[
            pl.BlockSpec((PROJ_ROWS, d), lambda i: (i, 0)),
            pl.BlockSpec((PROJ_ROWS, x.shape[1]), lambda i: (i, 0)),
            _resident(w.shape),
            _resident(g.shape),
            _resident(b.shape),
        ],
        out_specs=pl.BlockSpec((PROJ_ROWS, d), lambda i: (i, 0)),
        out_shape=jax.ShapeDtypeStruct((m, d), F32),
        compiler_params=_params(("parallel",)),
        name="proj_ln",
    )(h, x, w, g, b)


QKV_ROWS = 512
BF16_SUBLANES = 16
MOBA_VT_ROWS = HEAD_DIM + BF16_SUBLANES
MOBA_EXP2_SCALE = (HEAD_DIM ** -0.5) * math.log2(math.e)


def _qkv_rope_kernel(h_ref, wqk_ref, wvt_ref, cos_ref, sin_ref, q_ref, k_ref, vt_ref):
    hb = h_ref[...].astype(BF16)
    cos = cos_ref[...]
    sin = sin_ref[...]
    for part, out_ref, gain in ((0, q_ref, MOBA_EXP2_SCALE), (1, k_ref, 1.0)):
        x = _dot(hb, wqk_ref[:, part * D_MODEL:(part + 1) * D_MODEL])
        for hh in range(MOBA_HEADS):
            xs = x[:, hh * HEAD_DIM:(hh + 1) * HEAD_DIM]
            rot = pltpu.roll(xs, HEAD_DIM // 2, axis=1)
            y = xs * cos + rot * sin
            if gain != 1.0:
                y = y * gain
            out_ref[:, hh * HEAD_DIM:(hh + 1) * HEAD_DIM] = y.astype(BF16)
    vt = _dot_nt(wvt_ref[...], hb).astype(BF16)
    ones = jnp.ones((MOBA_VT_ROWS - HEAD_DIM, MOBA_BLOCK), BF16)
    for hh in range(MOBA_HEADS):
        for blk in range(QKV_ROWS // MOBA_BLOCK):
            vt_ref[hh, blk, 0:HEAD_DIM, :] = vt[hh * HEAD_DIM:(hh + 1) * HEAD_DIM,
                                                blk * MOBA_BLOCK:(blk + 1) * MOBA_BLOCK]
            vt_ref[hh, blk, HEAD_DIM:MOBA_VT_ROWS, :] = ones


def _rope_tables(seq):
    half = HEAD_DIM // 2
    inv_freq = ROPE_THETA ** (-jnp.arange(half, dtype=F32) / half)
    ang = jnp.arange(seq).astype(F32)[:, None] * inv_freq[None, :]
    cos = jnp.cos(ang)
    sin = jnp.sin(ang)
    return jnp.concatenate([cos, cos], axis=-1), jnp.concatenate([-sin, sin], axis=-1)


def _qkv_rope(h3, wqk, wvt, cos, sin):
    batch, seq, d = h3.shape
    tm = QKV_ROWS
    blocks = tm // MOBA_BLOCK
    row_spec = pl.BlockSpec((None, tm, d), lambda b, t: (b, t, 0))
    tab_spec = pl.BlockSpec((tm, HEAD_DIM), lambda b, t: (t, 0))
    vt_spec = pl.BlockSpec((None, MOBA_HEADS, blocks, MOBA_VT_ROWS, MOBA_BLOCK), lambda b, t: (b, 0, t, 0, 0))
    qk_out = jax.ShapeDtypeStruct((batch, seq, d), BF16)
    vt_out = jax.ShapeDtypeStruct((batch, MOBA_HEADS, seq // MOBA_BLOCK, MOBA_VT_ROWS, MOBA_BLOCK), BF16)
    return pl.pallas_call(
        _qkv_rope_kernel,
        grid=(batch, seq // tm),
        in_specs=[row_spec, _resident(wqk.shape), _resident(wvt.shape), tab_spec, tab_spec],
        out_specs=[row_spec, row_spec, vt_spec],
        out_shape=[qk_out, qk_out, vt_out],
        compiler_params=_params(("parallel", "parallel")),
        name="moba_qkv_rope",
    )(h3, wqk, wvt, cos, sin)


MOBA_TILES_PER_STEP = 8


def _moba_attn_kernel(q_ref, k_ref, vt_ref, o_ref, kmean_ref, *, n_blk, n_pad, per_step):
    step = pl.program_id(2)
    bs = MOBA_BLOCK

    @pl.when(step == 0)
    def _():
        kmean_ref[...] = jnp.zeros_like(kmean_ref)
        for n in range(n_blk):
            kb = k_ref[n * bs:(n + 1) * bs, :].astype(F32)
            kmean_ref[n:n + 1, :] = jnp.sum(kb, axis=0, keepdims=True) * (1.0 / bs)

    def tile(n_past):
        q = q_ref[n_past * bs:(n_past + 1) * bs, :]
        if n_past > 0:
            gate = _dot_nt(kmean_ref[...].astype(BF16), q)
        s = _dot_nt(k_ref[n_past * bs:(n_past + 1) * bs, :], q)
        if n_past > 0:
            s_past = _dot_nt(k_ref[0:n_past * bs, :], q)
        yield

        key = lax.broadcasted_iota(jnp.int32, s.shape, 0)
        qry = lax.broadcasted_iota(jnp.int32, s.shape, 1)
        s = jnp.where(key <= qry, s, NEG_INF)
        m = jnp.max(s, axis=0, keepdims=True)
        p = jnp.exp2(s - m)
        acc = _dot(vt_ref[n_past], p.astype(BF16))
        yield

        if n_past > 0:
            blk_f = lax.broadcasted_iota(jnp.int32, gate.shape, 0).astype(F32)
            past = blk_f < float(n_past)
            gate = jnp.where(past, gate, NEG_INF)
            sel = jnp.zeros(gate.shape, F32)
            for _ in range(min(MOBA_TOPK, n_blk)):
                top = jnp.max(gate, axis=0, keepdims=True)
                first = jnp.min(jnp.where(gate == top, blk_f, float(n_pad)), axis=0, keepdims=True)
                pick = blk_f == first
                sel = jnp.where(jnp.logical_and(pick, past), 1.0, sel)
                gate = jnp.where(pick, -jnp.inf, gate)

            for n in range(n_past):
                chosen = sel[n:n + 1, :] > 0.5
                s = s_past[n * bs:(n + 1) * bs]
                m_blk = jnp.where(chosen, jnp.max(s, axis=0, keepdims=True), NEG_INF)
                m_new = jnp.maximum(m, m_blk)
                corr = jnp.exp2(m - m_new)
                p = jnp.exp2(s - jnp.where(chosen, m_new, jnp.inf))
                acc = acc * corr + _dot(vt_ref[n], p.astype(BF16))
                m = m_new
                yield

        o = acc[0:HEAD_DIM] / acc[HEAD_DIM:HEAD_DIM + 1]
        o_ref[n_past * bs:(n_past + 1) * bs, :] = o.T.astype(o_ref.dtype)

    def tile_group(i):
        half = per_step // 2
        per_half = n_blk // per_step
        tiles = []
        for g in range(half):
            lo = g * 2 * per_half + i
            tiles += [n_blk - 1 - lo, lo]
        progs = [tile(n) for n in sorted(tiles, reverse=True)]
        while progs:
            progs = [prog for prog in progs if next(prog, "done") != "done"]

    for i in range(n_blk // per_step):
        pl.when(step == i)(functools.partial(tile_group, i))


def _moba_attention(q, k, vt):
    batch, seq, _ = q.shape
    n_blk = seq // MOBA_BLOCK
    n_pad = -(-n_blk // SUBLANES) * SUBLANES
    per_step = min(MOBA_TILES_PER_STEP, n_blk)
    assert seq % MOBA_BLOCK == 0 and per_step % 2 == 0 and n_blk % per_step == 0
    q_spec = pl.BlockSpec((None, seq, HEAD_DIM), lambda b, h, i: (b, 0, h))
    k_spec = q_spec
    vt_spec = pl.BlockSpec((None, None, n_blk, MOBA_VT_ROWS, MOBA_BLOCK), lambda b, h, i: (b, h, 0, 0, 0))
    return pl.pallas_call(
        functools.partial(_moba_attn_kernel, n_blk=n_blk, n_pad=n_pad, per_step=per_step),
        grid=(batch, MOBA_HEADS, n_blk // per_step),
        in_specs=[q_spec, k_spec, vt_spec],
        out_specs=q_spec,
        out_shape=jax.ShapeDtypeStruct(q.shape, BF16),
        scratch_shapes=[
            pltpu.VMEM((n_pad, HEAD_DIM), F32),
        ],
        compiler_params=_params(("parallel", "parallel", "arbitrary")),
        name="moba_attn",
    )(q, k, vt)


GDN_IN_ROWS = 256


def _gdn_in_kernel(h_ref, wqkv_ref, wb_ref, wa_ref, cw_ref, alog_ref, dtb_ref,
                   q_ref, k_ref, v_ref, beta_ref, gc_ref, xs_ref):
    tm = GDN_IN_ROWS
    halo = SUBLANES
    hb = h_ref[...].astype(BF16)

    @pl.when(pl.program_id(1) == 0)
    def _():
        xs_ref[0:halo, :] = jnp.zeros((halo, GDN_CONV_DIM), F32)

    xs_ref[halo:halo + tm, :] = _dot(hb, wqkv_ref[...])

    n_cols = GDN_CONV_DIM // HEAD_DIM
    for c in range(n_cols):
        cs = slice(c * HEAD_DIM, (c + 1) * HEAD_DIM)
        xe = xs_ref[:, cs]
        y = xe[halo:] * cw_ref[GDN_CONV - 1:GDN_CONV, cs]
        for back in range(1, GDN_CONV):
            i = GDN_CONV - 1 - back
            y = y + pltpu.roll(xe, back, axis=0)[halo:] * cw_ref[i:i + 1, cs]
        y = y * _sigmoid(y)
        if c < 2 * GDN_QK_HEADS:
            y = y * lax.rsqrt(jnp.sum(y * y, axis=-1, keepdims=True) + RMS_EPS)
            if c < GDN_QK_HEADS:
                q_ref[c] = (y * (HEAD_DIM ** -0.5)).astype(BF16)
            else:
                k_ref[c - GDN_QK_HEADS] = y.astype(BF16)
        else:
            v_ref[c - 2 * GDN_QK_HEADS] = y.astype(BF16)

    xs_ref[0:halo, :] = xs_ref[tm:tm + halo, :]

    beta = _sigmoid(_dot(hb, wb_ref[...]))
    a = _dot(hb, wa_ref[...]) + dtb_ref[...]
    softplus = jnp.maximum(a, 0.0) + jnp.log1p(jnp.exp(-jnp.abs(a)))
    g = -jnp.exp(alog_ref[...]) * softplus
    ri = lax.broadcasted_iota(jnp.int32, (tm, tm), 0)
    ci = lax.broadcasted_iota(jnp.int32, (tm, tm), 1)
    tri = jnp.logical_and(ri >= ci, ri // GDN_CHUNK == ci // GDN_CHUNK).astype(F32)
    gc = jnp.dot(tri, g, preferred_element_type=F32, precision=lax.Precision.HIGHEST)
    beta_ref[...] = beta[:, :GDN_V_HEADS]
    gc_ref[...] = gc[:, :GDN_V_HEADS]


def _gdn_in(h3, wqkv, wb, wa, conv_w, alog, dtb):
    batch, seq, d = h3.shape
    tm = GDN_IN_ROWS

    def rows(width):
        return pl.BlockSpec((None, tm, width), lambda b, t: (b, t, 0))

    def heads(n):
        return pl.BlockSpec((None, n, tm, HEAD_DIM), lambda b, t: (b, 0, t, 0))

    def head_out(n):
        return jax.ShapeDtypeStruct((batch, n, seq, HEAD_DIM), BF16)

    gate_out = jax.ShapeDtypeStruct((batch, seq, GDN_V_HEADS), F32)
    return pl.pallas_call(
        _gdn_in_kernel,
        grid=(batch, seq // tm),
        in_specs=[rows(d), _resident(wqkv.shape), _resident(wb.shape), _resident(wa.shape),
                  _resident(conv_w.shape), _resident(alog.shape), _resident(dtb.shape)],
        out_specs=[heads(GDN_QK_HEADS), heads(GDN_QK_HEADS), heads(GDN_V_HEADS),
                   rows(GDN_V_HEADS), rows(GDN_V_HEADS)],
        out_shape=[head_out(GDN_QK_HEADS), head_out(GDN_QK_HEADS), head_out(GDN_V_HEADS),
                   gate_out, gate_out],
        scratch_shapes=[pltpu.VMEM((tm + SUBLANES, GDN_CONV_DIM), F32)],
        compiler_params=_params(("parallel", "arbitrary")),
        name="gdn_in",
    )(h3, wqkv, wb, wa, conv_w, alog, dtb)


GDN_TILE = 256


def _gdn_core_kernel(q_ref, k_ref, v_ref, gcol_ref, bcol_ref, grow_ref, o_ref,
                     s_ref, u_ref, wq_ref, kd_ref, attn_ref, gl_ref, bd_ref):
    tile = GDN_TILE
    c_len = GDN_CHUNK
    n_chunks = tile // c_len
    group = GDN_V_HEADS // GDN_QK_HEADS

    @pl.when(pl.program_id(1) == 0)
    def _():
        s_ref[...] = jnp.zeros_like(s_ref)

    ii = lax.broadcasted_iota(jnp.int32, (tile, tile), 0)
    jj = lax.broadcasted_iota(jnp.int32, (tile, tile), 1)
    bd_ref[...] = ((ii // c_len) == (jj // c_len)).astype(BF16)
    ci = lax.broadcasted_iota(jnp.int32, (c_len, tile), 0)
    cj = lax.broadcasted_iota(jnp.int32, (c_len, tile), 1)
    cj_in = cj % c_len
    cj_chunk = cj // c_len
    cat_lower = ci >= cj_in
    cat_strict = ci > cj_in
    eye_cat = (ci == cj_in).astype(F32)
    half_lane = lax.broadcasted_iota(jnp.int32, (c_len, LANES), 1) < c_len
    head_lane = lax.broadcasted_iota(jnp.int32, (tile, GDN_V_HEADS), 1)

    def diag_blocks(x_full):
        out = x_full[0:c_len]
        for c in range(1, n_chunks):
            out = jnp.where(cj_chunk == c, x_full[c * c_len:(c + 1) * c_len], out)
        return out

    def rows_to_cat(x_b):
        halves = []
        for c in range(0, n_chunks, 2):
            halves.append(jnp.where(half_lane, x_b[c * c_len:(c + 1) * c_len],
                                    x_b[(c + 1) * c_len:(c + 2) * c_len]))
        return jnp.concatenate(halves, axis=1)

    def expand(x_cat):
        return jnp.concatenate([x_cat] * n_chunks, axis=0) * bd_ref[...]

    def prepare(hk, q_t, k_t, kk_cat, qk_cat, j):
        hv = hk * group + j
        pick = head_lane == hv
        gcc = jnp.sum(jnp.where(pick, gcol_ref[...], 0.0), axis=-1, keepdims=True)
        beta = jnp.sum(jnp.where(pick, bcol_ref[...], 0.0), axis=-1, keepdims=True)
        gcc_b = jnp.broadcast_to(gcc, (tile, LANES))
        beta_b = jnp.broadcast_to(beta, (tile, LANES))
        gcr = grow_ref[hv]
        decay = jnp.exp(jnp.where(cat_lower, rows_to_cat(gcc_b) - gcr, NEG_INF))
        attn_cat = (qk_cat * decay).astype(BF16)
        for c in range(n_chunks):
            attn_ref[hv, c] = attn_cat[:, c * c_len:(c + 1) * c_len]
        p_cat = jnp.where(cat_strict, -(rows_to_cat(beta_b) * kk_cat * decay), 0.0)
        gam_b = jnp.exp(gcc_b)
        k32 = k_t.astype(F32)
        v32 = v_ref[hv].astype(F32)
        rhs = jnp.concatenate([v32 * beta_b, k32 * (beta_b * gam_b)], axis=1).astype(BF16)
        q_dec = (q_t.astype(F32) * gam_b).astype(BF16)
        g_last_b = jnp.concatenate(
            [jnp.broadcast_to(gcc_b[(c + 1) * c_len - 1:(c + 1) * c_len], (c_len, LANES))
             for c in range(n_chunks)], axis=0)
        kd_ref[hv] = (k32 * jnp.exp(g_last_b - gcc_b)).astype(BF16)
        for c in range(n_chunks):
            wq_ref[hv, (2 * c + 1) * c_len:(2 * c + 2) * c_len] = q_dec[c * c_len:(c + 1) * c_len]
            gl_ref[hv, c] = jnp.exp(g_last_b[c * c_len:c * c_len + SUBLANES])
        return dict(hv=hv, p_bd=expand(p_cat.astype(BF16)), p_cat=p_cat, t_cat=eye_cat + p_cat, rhs=rhs)

    def head_program(hk, j, shared):
        if not shared:
            q_t = q_ref[hk]
            k_t = k_ref[hk]
            shared.update(q_t=q_t, k_t=k_t, kk_cat=diag_blocks(_dot_nt(k_t, k_t)),
                          qk_cat=diag_blocks(_dot_nt(q_t, k_t)))
        h = prepare(hk, shared["q_t"], shared["k_t"], shared["kk_cat"], shared["qk_cat"], j)
        yield
        p_cat = _dot(h["p_cat"].astype(BF16), h["p_bd"])
        t_cat = h["t_cat"]
        yield
        n = 2
        while 2 * n < c_len:
            both = _dot(jnp.concatenate([t_cat, p_cat], axis=0).astype(BF16), expand(p_cat.astype(BF16)))
            t_cat = t_cat + both[:c_len]
            p_cat = both[c_len:]
            n *= 2
            yield
        t_cat = t_cat + _dot(t_cat.astype(BF16), expand(p_cat.astype(BF16)))
        yield
        uw = _dot(expand(t_cat.astype(BF16)), h["rhs"])
        hv = h["hv"]
        u_ref[hv] = uw[:, :HEAD_DIM]
        w_b = uw[:, HEAD_DIM:].astype(BF16)
        for c in range(n_chunks):
            wq_ref[hv, 2 * c * c_len:(2 * c + 1) * c_len] = w_b[c * c_len:(c + 1) * c_len]

    progs = []
    for hk in range(GDN_QK_HEADS):
        shared = {}
        for j in range(group):
            progs.append(head_program(hk, j, shared))
    while progs:
        progs = [prog for prog in progs if next(prog, "done") != "done"]

    def phase2(c, carry):
        r0 = pl.multiple_of(c * c_len, c_len)
        r2 = pl.multiple_of(c * 2 * c_len, 2 * c_len)
        boths = [_dot(wq_ref[hv, pl.ds(r2, 2 * c_len), :], s_ref[hv].astype(BF16))
                 for hv in range(GDN_V_HEADS)]
        for hv in range(GDN_V_HEADS):
            v_new = (u_ref[hv, pl.ds(r0, c_len), :] - boths[hv][:c_len]).astype(BF16)
            o_ref[pl.ds(r0, c_len), hv * HEAD_DIM:(hv + 1) * HEAD_DIM] = (
                boths[hv][c_len:] + _dot(attn_ref[hv, c], v_new)).astype(o_ref.dtype)
            s_ref[hv] = s_ref[hv] * gl_ref[hv, c][0:1, :] + lax.dot_general(
                kd_ref[hv, pl.ds(r0, c_len), :], v_new, TN_DIMS, preferred_element_type=F32)
        return carry

    lax.fori_loop(0, n_chunks, phase2, 0)


def _gdn_core(q, k, v, gc, beta):
    batch, _, seq, _ = q.shape
    tile = GDN_TILE
    n_chunks = tile // GDN_CHUNK
    nh = GDN_V_HEADS
    g_row = gc.transpose(0, 2, 1).reshape(batch, nh, seq // tile, 1, tile)

    def heads(n):
        return pl.BlockSpec((None, n, tile, HEAD_DIM), lambda b, t: (b, 0, t, 0))

    col_spec = pl.BlockSpec((None, tile, nh), lambda b, t: (b, t, 0))
    row_spec = pl.BlockSpec((None, nh, None, 1, tile), lambda b, t: (b, 0, t, 0, 0))
    return pl.pallas_call(
        _gdn_core_kernel,
        grid=(batch, seq // tile),
        in_specs=[heads(GDN_QK_HEADS), heads(GDN_QK_HEADS), heads(nh), col_spec, col_spec, row_spec],
        out_specs=pl.BlockSpec((None, tile, GDN_V_DIM), lambda b, t: (b, t, 0)),
        out_shape=jax.ShapeDtypeStruct((batch, seq, GDN_V_DIM), BF16),
        scratch_shapes=[
            pltpu.VMEM((nh, HEAD_DIM, HEAD_DIM), F32),
            pltpu.VMEM((nh, tile, HEAD_DIM), F32),
            pltpu.VMEM((nh, 2 * tile, HEAD_DIM), BF16),
            pltpu.VMEM((nh, tile, HEAD_DIM), BF16),
            pltpu.VMEM((nh, n_chunks, GDN_CHUNK, GDN_CHUNK), BF16),
            pltpu.VMEM((nh, n_chunks, SUBLANES, LANES), F32),
            pltpu.VMEM((tile, tile), BF16),
        ],
        compiler_params=_params(("parallel", "arbitrary")),
        name="gdn_core",
    )(q, k, v, gc, beta, g_row)


GDN_OUT_ROWS = 512


def _gdn_out_kernel(h_ref, o_ref, wz_ref, nw_ref, wo_ref, g_ref, b_ref, out_ref, y_ref):
    h = h_ref[...]
    z = _dot(h.astype(BF16), wz_ref[...])
    nw = nw_ref[...]
    for n in range(GDN_V_HEADS):
        cs = slice(n * HEAD_DIM, (n + 1) * HEAD_DIM)
        o = o_ref[:, cs].astype(F32)
        zz = z[:, cs]
        y = o * lax.rsqrt(jnp.mean(o * o, axis=-1, keepdims=True) + RMS_EPS) * nw * (zz * _sigmoid(zz))
        y_ref[:, cs] = y.astype(BF16)
    y = ALPHA * h + _dot(y_ref[...], wo_ref[...])
    out_ref[...] = _layer_norm(y, g_ref[...], b_ref[...])


def _gdn_out(h, o, wz, nw, wo, g, b):
    m, d = h.shape
    tm = GDN_OUT_ROWS
    return pl.pallas_call(
        _gdn_out_kernel,
        grid=(m // tm,),
        in_specs=[pl.BlockSpec((tm, d), lambda i: (i, 0)),
                  pl.BlockSpec((tm, GDN_V_DIM), lambda i: (i, 0)),
                  _resident(wz.shape), _resident(nw.shape), _resident(wo.shape),
                  _resident(g.shape), _resident(b.shape)],
        out_specs=pl.BlockSpec((tm, d), lambda i: (i, 0)),
        out_shape=jax.ShapeDtypeStruct((m, d), F32),
        scratch_shapes=[pltpu.VMEM((tm, GDN_V_DIM), BF16)],
        compiler_params=_params(("parallel",)),
        name="gdn_out",
    )(h, o, wz, nw, wo, g, b)


def _row(vec):
    return vec.reshape(1, -1).astype(F32)


def _pad_lanes(x2d):
    return jnp.pad(x2d, ((0, 0), (0, LANES - x2d.shape[1])))


def kernel(x, ln_g, ln_b, ffn_pre_w_in, ffn_pre_w_out, ffn_post_w_in, ffn_post_w_out,
           moba_w_in, moba_w_out, gdn_w_in, gdn_conv_w, gdn_a_log, gdn_dt_bias,
           gdn_norm_w, gdn_w_out):
    batch, seq, d = x.shape
    m = batch * seq
    h = x.reshape(m, d).astype(F32)
    cos, sin = _rope_tables(seq)

    for i in range(DEPTH):
        h = _ffn_block(h, ffn_pre_w_in[i].astype(BF16), ffn_pre_w_out[i].astype(BF16),
                       _row(ln_g[i, 0]), _row(ln_b[i, 0]))
        j = i // 2
        if i % 2 == 0:
            w_in = moba_w_in[j]
            q, k, vt = _qkv_rope(h.reshape(batch, seq, d), w_in[:, :2 * D_MODEL].astype(BF16),
                                 w_in[:, 2 * D_MODEL:].T.astype(BF16), cos, sin)
            o = _moba_attention(q, k, vt)
            h = _proj_ln(h, o.reshape(m, d), moba_w_out[j].astype(BF16),
                         _row(ln_g[i, 1]), _row(ln_b[i, 1]))
        else:
            w_in = gdn_w_in[j]
            z0 = GDN_CONV_DIM
            b0 = z0 + GDN_V_DIM
            a0 = b0 + GDN_V_HEADS
            wqkv = w_in[:, :z0].astype(BF16)
            wz = w_in[:, z0:b0].astype(BF16)
            wb = _pad_lanes(w_in[:, b0:a0]).astype(BF16)
            wa = _pad_lanes(w_in[:, a0:a0 + GDN_V_HEADS]).astype(BF16)
            q, k, v, beta, gc = _gdn_in(
                h.reshape(batch, seq, d), wqkv, wb, wa, gdn_conv_w[j].astype(F32),
                _pad_lanes(_row(gdn_a_log[j])), _pad_lanes(_row(gdn_dt_bias[j])))
            o = _gdn_core(q, k, v, gc, beta)
            h = _gdn_out(h, o.reshape(m, GDN_V_DIM), wz, _row(gdn_norm_w[j]),
                         gdn_w_out[j].astype(BF16), _row(ln_g[i, 1]), _row(ln_b[i, 1]))
        h = _ffn_block(h, ffn_post_w_in[i].astype(BF16), ffn_post_w_out[i].astype(BF16),
                       _row(ln_g[i, 2]), _row(ln_b[i, 2]))
    return h.reshape(batch, seq, d).astype(x.dtype)
```

```python
import functools
import math

import jax
import jax.numpy as jnp
from jax import lax
from jax.experimental import pallas as pl
from jax.experimental.pallas import tpu as pltpu

F32 = jnp.float32
BF16 = jnp.bfloat16

D_MODEL = 1024
DEPTH = 2
D_FF = 2816
HEAD_DIM = 128
MOBA_HEADS = D_MODEL // HEAD_DIM
MOBA_BLOCK = 256
MOBA_TOPK = 3
ROPE_THETA = 10000.0
GDN_QK_HEADS = D_MODEL // HEAD_DIM
GDN_V_HEADS = 2 * GDN_QK_HEADS
GDN_QK_DIM = GDN_QK_HEADS * HEAD_DIM
GDN_V_DIM = GDN_V_HEADS * HEAD_DIM
GDN_CONV_DIM = 2 * GDN_QK_DIM + GDN_V_DIM
GDN_CONV = 4
GDN_CHUNK = 64
ALPHA = (2 * DEPTH) ** 0.25
HALF_STEP = 0.5
LN_EPS = 1e-5
RMS_EPS = 1e-6
NEG_INF = -1e30

LANES = 128
SUBLANES = 8
VMEM_LIMIT = 56 * 1024 * 1024

NT_DIMS = (((1,), (1,)), ((), ()))
TN_DIMS = (((0,), (0,)), ((), ()))


def _dot(a, b):
    return jnp.dot(a, b, preferred_element_type=F32)


def _dot_nt(a, b):
    return lax.dot_general(a, b, NT_DIMS, preferred_element_type=F32)


def _sigmoid(x):
    return 1.0 / (1.0 + jnp.exp(-x))


def _layer_norm(y, g, b):
    mu = jnp.mean(y, axis=-1, keepdims=True)
    d = y - mu
    var = jnp.mean(d * d, axis=-1, keepdims=True)
    return d * lax.rsqrt(var + LN_EPS) * g + b


def _resident(shape):
    nd = len(shape)
    return pl.BlockSpec(shape, lambda *_: (0,) * nd, pipeline_mode=pl.Buffered(1))


def _params(semantics):
    return pltpu.CompilerParams(dimension_semantics=semantics, vmem_limit_bytes=VMEM_LIMIT)


FFN_ROWS = 1024
FFN_SUBTILES = 4


def _ffn_kernel(h_ref, win_ref, wout_ref, g_ref, b_ref, o_ref):
    sub = FFN_ROWS // FFN_SUBTILES
    for r in range(FFN_SUBTILES):
        rows = slice(r * sub, (r + 1) * sub)
        h = h_ref[rows, :]
        hb = h.astype(BF16)
        gate = _dot(hb, win_ref[:, :D_FF])
        up = _dot(hb, win_ref[:, D_FF:])
        act = (gate * _sigmoid(gate) * up).astype(BF16)
        y = ALPHA * h + HALF_STEP * _dot(act, wout_ref[...])
        o_ref[rows, :] = _layer_norm(y, g_ref[...], b_ref[...])


def _ffn_block(h, w_in, w_out, g, b):
    m, d = h.shape
    return pl.pallas_call(
        _ffn_kernel,
        grid=(m // FFN_ROWS,),
        in_specs=[
            pl.BlockSpec((FFN_ROWS, d), lambda i: (i, 0)),
            _resident(w_in.shape),
            _resident(w_out.shape),
            _resident(g.shape),
            _resident(b.shape),
        ],
        out_specs=pl.BlockSpec((FFN_ROWS, d), lambda i: (i, 0)),
        out_shape=jax.ShapeDtypeStruct((m, d), F32),
        compiler_params=_params(("parallel",)),
        name="ffn_ln",
    )(h, w_in, w_out, g, b)


QKV_ROWS = 512
BF16_SUBLANES = 16
MOBA_VT_ROWS = HEAD_DIM + BF16_SUBLANES
MOBA_EXP2_SCALE = (HEAD_DIM ** -0.5) * math.log2(math.e)


def _qkv_rope_kernel(h_ref, wqk_ref, wvt_ref, cos_ref, sin_ref, q_ref, k_ref, vt_ref):
    hb = h_ref[...].astype(BF16)
    cos = cos_ref[...]
    sin = sin_ref[...]
    for part, out_ref, gain in ((0, q_ref, MOBA_EXP2_SCALE), (1, k_ref, 1.0)):
        x = _dot(hb, wqk_ref[:, part * D_MODEL:(part + 1) * D_MODEL])
        for hh in range(MOBA_HEADS):
            xs = x[:, hh * HEAD_DIM:(hh + 1) * HEAD_DIM]
            rot = pltpu.roll(xs, HEAD_DIM // 2, axis=1)
            y = xs * cos + rot * sin
            if gain != 1.0:
                y = y * gain
            out_ref[:, hh * HEAD_DIM:(hh + 1) * HEAD_DIM] = y.astype(BF16)
    vt = _dot_nt(wvt_ref[...], hb).astype(BF16)
    ones = jnp.ones((MOBA_VT_ROWS - HEAD_DIM, MOBA_BLOCK), BF16)
    for hh in range(MOBA_HEADS):
        for blk in range(QKV_ROWS // MOBA_BLOCK):
            vt_ref[hh, blk, 0:HEAD_DIM, :] = vt[hh * HEAD_DIM:(hh + 1) * HEAD_DIM,
                                                blk * MOBA_BLOCK:(blk + 1) * MOBA_BLOCK]
            vt_ref[hh, blk, HEAD_DIM:MOBA_VT_ROWS, :] = ones


def _rope_tables(seq):
    half = HEAD_DIM // 2
    inv_freq = ROPE_THETA ** (-jnp.arange(half, dtype=F32) / half)
    ang = jnp.arange(seq).astype(F32)[:, None] * inv_freq[None, :]
    cos = jnp.cos(ang)
    sin = jnp.sin(ang)
    return jnp.concatenate([cos, cos], axis=-1), jnp.concatenate([-sin, sin], axis=-1)


def _qkv_rope(h3, wqk, wvt, cos, sin):
    batch, seq, d = h3.shape
    tm = QKV_ROWS
    blocks = tm // MOBA_BLOCK
    row_spec = pl.BlockSpec((None, tm, d), lambda b, t: (b, t, 0))
    tab_spec = pl.BlockSpec((tm, HEAD_DIM), lambda b, t: (t, 0))
    vt_spec = pl.BlockSpec((None, MOBA_HEADS, blocks, MOBA_VT_ROWS, MOBA_BLOCK), lambda b, t: (b, 0, t, 0, 0))
    qk_out = jax.ShapeDtypeStruct((batch, seq, d), BF16)
    vt_out = jax.ShapeDtypeStruct((batch, MOBA_HEADS, seq // MOBA_BLOCK, MOBA_VT_ROWS, MOBA_BLOCK), BF16)
    return pl.pallas_call(
        _qkv_rope_kernel,
        grid=(batch, seq // tm),
        in_specs=[row_spec, _resident(wqk.shape), _resident(wvt.shape), tab_spec, tab_spec],
        out_specs=[row_spec, row_spec, vt_spec],
        out_shape=[qk_out, qk_out, vt_out],
        compiler_params=_params(("parallel", "parallel")),
        name="moba_qkv_rope",
    )(h3, wqk, wvt, cos, sin)


MOBA_TILES_PER_STEP = 8


def _moba_attn_kernel(q_ref, k_ref, vt_ref, o_ref, kmean_ref, *, n_blk, n_pad, per_step):
    step = pl.program_id(2)
    bs = MOBA_BLOCK

    @pl.when(step == 0)
    def _():
        kmean_ref[...] = jnp.zeros_like(kmean_ref)
        for n in range(n_blk):
            kb = k_ref[n * bs:(n + 1) * bs, :].astype(F32)
            kmean_ref[n:n + 1, :] = jnp.sum(kb, axis=0, keepdims=True) * (1.0 / bs)

    def tile(n_past):
        q = q_ref[n_past * bs:(n_past + 1) * bs, :]
        if n_past > 0:
            gate = _dot_nt(kmean_ref[...].astype(BF16), q)
        s = _dot_nt(k_ref[n_past * bs:(n_past + 1) * bs, :], q)
        if n_past > 0:
            s_past = _dot_nt(k_ref[0:n_past * bs, :], q)
        yield

        key = lax.broadcasted_iota(jnp.int32, s.shape, 0)
        qry = lax.broadcasted_iota(jnp.int32, s.shape, 1)
        s = jnp.where(key <= qry, s, NEG_INF)
        m = jnp.max(s, axis=0, keepdims=True)
        p = jnp.exp2(s - m)
        acc = _dot(vt_ref[n_past], p.astype(BF16))
        yield

        if n_past > 0:
            blk_f = lax.broadcasted_iota(jnp.int32, gate.shape, 0).astype(F32)
            past = blk_f < float(n_past)
            gate = jnp.where(past, gate, NEG_INF)
            sel = jnp.zeros(gate.shape, F32)
            for _ in range(min(MOBA_TOPK, n_blk)):
                top = jnp.max(gate, axis=0, keepdims=True)
                first = jnp.min(jnp.where(gate == top, blk_f, float(n_pad)), axis=0, keepdims=True)
                pick = blk_f == first
                sel = jnp.where(jnp.logical_and(pick, past), 1.0, sel)
                gate = jnp.where(pick, -jnp.inf, gate)

            for n in range(n_past):
                chosen = sel[n:n + 1, :] > 0.5
                s = s_past[n * bs:(n + 1) * bs]
                m_blk = jnp.where(chosen, jnp.max(s, axis=0, keepdims=True), NEG_INF)
                m_new = jnp.maximum(m, m_blk)
                corr = jnp.exp2(m - m_new)
                p = jnp.exp2(s - jnp.where(chosen, m_new, jnp.inf))
                acc = acc * corr + _dot(vt_ref[n], p.astype(BF16))
                m = m_new
                yield

        o = acc[0:HEAD_DIM] / acc[HEAD_DIM:HEAD_DIM + 1]
        o_ref[n_past * bs:(n_past + 1) * bs, :] = o.T.astype(o_ref.dtype)

    def tile_group(i):
        half = per_step // 2
        per_half = n_blk // per_step
        tiles = []
        for g in range(half):
            lo = g * 2 * per_half + i
            tiles += [n_blk - 1 - lo, lo]
        progs = [tile(n) for n in sorted(tiles, reverse=True)]
        while progs:
            progs = [prog for prog in progs if next(prog, "done") != "done"]

    for i in range(n_blk // per_step):
        pl.when(step == i)(functools.partial(tile_group, i))


def _moba_attention(q, k, vt):
    batch, seq, _ = q.shape
    n_blk = seq // MOBA_BLOCK
    n_pad = -(-n_blk // SUBLANES) * SUBLANES
    per_step = min(MOBA_TILES_PER_STEP, n_blk)
    assert seq % MOBA_BLOCK == 0 and per_step % 2 == 0 and n_blk % per_step == 0
    q_spec = pl.BlockSpec((None, seq, HEAD_DIM), lambda b, h, i: (b, 0, h))
    k_spec = q_spec
    vt_spec = pl.BlockSpec((None, None, n_blk, MOBA_VT_ROWS, MOBA_BLOCK), lambda b, h, i: (b, h, 0, 0, 0))
    return pl.pallas_call(
        functools.partial(_moba_attn_kernel, n_blk=n_blk, n_pad=n_pad, per_step=per_step),
        grid=(batch, MOBA_HEADS, n_blk // per_step),
        in_specs=[q_spec, k_spec, vt_spec],
        out_specs=q_spec,
        out_shape=jax.ShapeDtypeStruct(q.shape, BF16),
        scratch_shapes=[
            pltpu.VMEM((n_pad, HEAD_DIM), F32),
        ],
        compiler_params=_params(("parallel", "parallel", "arbitrary")),
        name="moba_attn",
    )(q, k, vt)


GDN_IN_ROWS = 256


def _gdn_in_kernel(h_ref, wqkv_ref, wb_ref, wa_ref, cw_ref, alog_ref, dtb_ref,
                   q_ref, k_ref, v_ref, beta_ref, gc_ref, xs_ref):
    tm = GDN_IN_ROWS
    halo = SUBLANES
    hb = h_ref[...].astype(BF16)

    @pl.when(pl.program_id(1) == 0)
    def _():
        xs_ref[0:halo, :] = jnp.zeros((halo, GDN_CONV_DIM), F32)

    xs_ref[halo:halo + tm, :] = _dot(hb, wqkv_ref[...])

    n_cols = GDN_CONV_DIM // HEAD_DIM
    for c in range(n_cols):
        cs = slice(c * HEAD_DIM, (c + 1) * HEAD_DIM)
        xe = xs_ref[:, cs]
        y = xe[halo:] * cw_ref[GDN_CONV - 1:GDN_CONV, cs]
        for back in range(1, GDN_CONV):
            i = GDN_CONV - 1 - back
            y = y + pltpu.roll(xe, back, axis=0)[halo:] * cw_ref[i:i + 1, cs]
        y = y * _sigmoid(y)
        if c < 2 * GDN_QK_HEADS:
            y = y * lax.rsqrt(jnp.sum(y * y, axis=-1, keepdims=True) + RMS_EPS)
            if c < GDN_QK_HEADS:
                q_ref[c] = (y * (HEAD_DIM ** -0.5)).astype(BF16)
            else:
                k_ref[c - GDN_QK_HEADS] = y.astype(BF16)
        else:
            v_ref[c - 2 * GDN_QK_HEADS] = y.astype(BF16)

    xs_ref[0:halo, :] = xs_ref[tm:tm + halo, :]

    beta = _sigmoid(_dot(hb, wb_ref[...]))
    a = _dot(hb, wa_ref[...]) + dtb_ref[...]
    softplus = jnp.maximum(a, 0.0) + jnp.log1p(jnp.exp(-jnp.abs(a)))
    g = -jnp.exp(alog_ref[...]) * softplus
    ri = lax.broadcasted_iota(jnp.int32, (tm, tm), 0)
    ci = lax.broadcasted_iota(jnp.int32, (tm, tm), 1)
    tri = jnp.logical_and(ri >= ci, ri // GDN_CHUNK == ci // GDN_CHUNK).astype(F32)
    gc = jnp.dot(tri, g, preferred_element_type=F32, precision=lax.Precision.HIGHEST)
    beta_ref[...] = beta[:, :GDN_V_HEADS]
    gc_ref[...] = gc[:, :GDN_V_HEADS]


def _gdn_in(h3, wqkv, wb, wa, conv_w, alog, dtb):
    batch, seq, d = h3.shape
    tm = GDN_IN_ROWS

    def rows(width):
        return pl.BlockSpec((None, tm, width), lambda b, t: (b, t, 0))

    def heads(n):
        return pl.BlockSpec((None, n, tm, HEAD_DIM), lambda b, t: (b, 0, t, 0))

    def head_out(n):
        return jax.ShapeDtypeStruct((batch, n, seq, HEAD_DIM), BF16)

    gate_out = jax.ShapeDtypeStruct((batch, seq, GDN_V_HEADS), F32)
    return pl.pallas_call(
        _gdn_in_kernel,
        grid=(batch, seq // tm),
        in_specs=[rows(d), _resident(wqkv.shape), _resident(wb.shape), _resident(wa.shape),
                  _resident(conv_w.shape), _resident(alog.shape), _resident(dtb.shape)],
        out_specs=[heads(GDN_QK_HEADS), heads(GDN_QK_HEADS), heads(GDN_V_HEADS),
                   rows(GDN_V_HEADS), rows(GDN_V_HEADS)],
        out_shape=[head_out(GDN_QK_HEADS), head_out(GDN_QK_HEADS), head_out(GDN_V_HEADS),
                   gate_out, gate_out],
        scratch_shapes=[pltpu.VMEM((tm + SUBLANES, GDN_CONV_DIM), F32)],
        compiler_params=_params(("parallel", "arbitrary")),
        name="gdn_in",
    )(h3, wqkv, wb, wa, conv_w, alog, dtb)


GDN_TILE = 256


def _gdn_core_kernel(q_ref, k_ref, v_ref, gcol_ref, bcol_ref, grow_ref, o_ref,
                     s_ref, u_ref, wq_ref, kd_ref, attn_ref, gl_ref, bd_ref):
    tile = GDN_TILE
    c_len = GDN_CHUNK
    n_chunks = tile // c_len
    group = GDN_V_HEADS // GDN_QK_HEADS

    @pl.when(pl.program_id(1) == 0)
    def _():
        s_ref[...] = jnp.zeros_like(s_ref)

    ii = lax.broadcasted_iota(jnp.int32, (tile, tile), 0)
    jj = lax.broadcasted_iota(jnp.int32, (tile, tile), 1)
    bd_ref[...] = ((ii // c_len) == (jj // c_len)).astype(BF16)
    ci = lax.broadcasted_iota(jnp.int32, (c_len, tile), 0)
    cj = lax.broadcasted_iota(jnp.int32, (c_len, tile), 1)
    cj_in = cj % c_len
    cj_chunk = cj // c_len
    cat_lower = ci >= cj_in
    cat_strict = ci > cj_in
    eye_cat = (ci == cj_in).astype(F32)
    half_lane = lax.broadcasted_iota(jnp.int32, (c_len, LANES), 1) < c_len
    head_lane = lax.broadcasted_iota(jnp.int32, (tile, GDN_V_HEADS), 1)

    def diag_blocks(x_full):
        out = x_full[0:c_len]
        for c in range(1, n_chunks):
            out = jnp.where(cj_chunk == c, x_full[c * c_len:(c + 1) * c_len], out)
        return out

    def rows_to_cat(x_b):
        halves = []
        for c in range(0, n_chunks, 2):
            halves.append(jnp.where(half_lane, x_b[c * c_len:(c + 1) * c_len],
                                    x_b[(c + 1) * c_len:(c + 2) * c_len]))
        return jnp.concatenate(halves, axis=1)

    def expand(x_cat):
        return jnp.concatenate([x_cat] * n_chunks, axis=0) * bd_ref[...]

    def prepare(hk, q_t, k_t, kk_cat, qk_cat, j):
        hv = hk * group + j
        pick = head_lane == hv
        gcc = jnp.sum(jnp.where(pick, gcol_ref[...], 0.0), axis=-1, keepdims=True)
        beta = jnp.sum(jnp.where(pick, bcol_ref[...], 0.0), axis=-1, keepdims=True)
        gcc_b = jnp.broadcast_to(gcc, (tile, LANES))
        beta_b = jnp.broadcast_to(beta, (tile, LANES))
        gcr = grow_ref[hv]
        decay = jnp.exp(jnp.where(cat_lower, rows_to_cat(gcc_b) - gcr, NEG_INF))
        attn_cat = (qk_cat * decay).astype(BF16)
        for c in range(n_chunks):
            attn_ref[hv, c] = attn_cat[:, c * c_len:(c + 1) * c_len]
        p_cat = jnp.where(cat_strict, -(rows_to_cat(beta_b) * kk_cat * decay), 0.0)
        gam_b = jnp.exp(gcc_b)
        k32 = k_t.astype(F32)
        v32 = v_ref[hv].astype(F32)
        rhs = jnp.concatenate([v32 * beta_b, k32 * (beta_b * gam_b)], axis=1).astype(BF16)
        q_dec = (q_t.astype(F32) * gam_b).astype(BF16)
        g_last_b = jnp.concatenate(
            [jnp.broadcast_to(gcc_b[(c + 1) * c_len - 1:(c + 1) * c_len], (c_len, LANES))
             for c in range(n_chunks)], axis=0)
        kd_ref[hv] = (k32 * jnp.exp(g_last_b - gcc_b)).astype(BF16)
        for c in range(n_chunks):
            wq_ref[hv, (2 * c + 1) * c_len:(2 * c + 2) * c_len] = q_dec[c * c_len:(c + 1) * c_len]
            gl_ref[hv, c] = jnp.exp(g_last_b[c * c_len:c * c_len + SUBLANES])
        return dict(hv=hv, p_bd=expand(p_cat.astype(BF16)), p_cat=p_cat, t_cat=eye_cat + p_cat, rhs=rhs)

    def head_program(hk, j, shared):
        if not shared:
            q_t = q_ref[hk]
            k_t = k_ref[hk]
            shared.update(q_t=q_t, k_t=k_t, kk_cat=diag_blocks(_dot_nt(k_t, k_t)),
                          qk_cat=diag_blocks(_dot_nt(q_t, k_t)))
        h = prepare(hk, shared["q_t"], shared["k_t"], shared["kk_cat"], shared["qk_cat"], j)
        yield
        p_cat = _dot(h["p_cat"].astype(BF16), h["p_bd"])
        t_cat = h["t_cat"]
        yield
        n = 2
        while 2 * n < c_len:
            both = _dot(jnp.concatenate([t_cat, p_cat], axis=0).astype(BF16), expand(p_cat.astype(BF16)))
            t_cat = t_cat + both[:c_len]
            p_cat = both[c_len:]
            n *= 2
            yield
        t_cat = t_cat + _dot(t_cat.astype(BF16), expand(p_cat.astype(BF16)))
        yield
        uw = _dot(expand(t_cat.astype(BF16)), h["rhs"])
        hv = h["hv"]
        u_ref[hv] = uw[:, :HEAD_DIM]
        w_b = uw[:, HEAD_DIM:].astype(BF16)
        for c in range(n_chunks):
            wq_ref[hv, 2 * c * c_len:(2 * c + 1) * c_len] = w_b[c * c_len:(c + 1) * c_len]

    progs = []
    for hk in range(GDN_QK_HEADS):
        shared = {}
        for j in range(group):
            progs.append(head_program(hk, j, shared))
    while progs:
        progs = [prog for prog in progs if next(prog, "done") != "done"]

    def phase2(c, carry):
        r0 = pl.multiple_of(c * c_len, c_len)
        r2 = pl.multiple_of(c * 2 * c_len, 2 * c_len)
        boths = [_dot(wq_ref[hv, pl.ds(r2, 2 * c_len), :], s_ref[hv].astype(BF16))
                 for hv in range(GDN_V_HEADS)]
        for hv in range(GDN_V_HEADS):
            v_new = (u_ref[hv, pl.ds(r0, c_len), :] - boths[hv][:c_len]).astype(BF16)
            o_ref[pl.ds(r0, c_len), hv * HEAD_DIM:(hv + 1) * HEAD_DIM] = (
                boths[hv][c_len:] + _dot(attn_ref[hv, c], v_new)).astype(o_ref.dtype)
            s_ref[hv] = s_ref[hv] * gl_ref[hv, c][0:1, :] + lax.dot_general(
                kd_ref[hv, pl.ds(r0, c_len), :], v_new, TN_DIMS, preferred_element_type=F32)
        return carry

    lax.fori_loop(0, n_chunks, phase2, 0)


def _gdn_core(q, k, v, gc, beta):
    batch, _, seq, _ = q.shape
    tile = GDN_TILE
    n_chunks = tile // GDN_CHUNK
    nh = GDN_V_HEADS
    g_row = gc.transpose(0, 2, 1).reshape(batch, nh, seq // tile, 1, tile)

    def heads(n):
        return pl.BlockSpec((None, n, tile, HEAD_DIM), lambda b, t: (b, 0, t, 0))

    col_spec = pl.BlockSpec((None, tile, nh), lambda b, t: (b, t, 0))
    row_spec = pl.BlockSpec((None, nh, None, 1, tile), lambda b, t: (b, 0, t, 0, 0))
    return pl.pallas_call(
        _gdn_core_kernel,
        grid=(batch, seq // tile),
        in_specs=[heads(GDN_QK_HEADS), heads(GDN_QK_HEADS), heads(nh), col_spec, col_spec, row_spec],
        out_specs=pl.BlockSpec((None, tile, GDN_V_DIM), lambda b, t: (b, t, 0)),
        out_shape=jax.ShapeDtypeStruct((batch, seq, GDN_V_DIM), BF16),
        scratch_shapes=[
            pltpu.VMEM((nh, HEAD_DIM, HEAD_DIM), F32),
            pltpu.VMEM((nh, tile, HEAD_DIM), F32),
            pltpu.VMEM((nh, 2 * tile, HEAD_DIM), BF16),
            pltpu.VMEM((nh, tile, HEAD_DIM), BF16),
            pltpu.VMEM((nh, n_chunks, GDN_CHUNK, GDN_CHUNK), BF16),
            pltpu.VMEM((nh, n_chunks, SUBLANES, LANES), F32),
            pltpu.VMEM((tile, tile), BF16),
        ],
        compiler_params=_params(("parallel", "arbitrary")),
        name="gdn_core",
    )(q, k, v, gc, beta, g_row)


def _run_pipelined(progs):
    pending = list(progs)
    active = []
    while pending or active:
        if pending:
            active.append(pending.pop(0))
        active = [prog for prog in active if next(prog, "done") != "done"]


def _ffn_stages(h1, win_ref, wout_ref, g_ref, b_ref, o_ref, rows):
    hb = h1.astype(BF16)
    gate = _dot(hb, win_ref[:, :D_FF])
    up = _dot(hb, win_ref[:, D_FF:])
    yield
    act = (gate * _sigmoid(gate) * up).astype(BF16)
    y = ALPHA * h1 + HALF_STEP * _dot(act, wout_ref[...])
    yield
    o_ref[rows, :] = _layer_norm(y, g_ref[...], b_ref[...])


MOBA_POST_ROWS = 1024
MOBA_POST_SUBTILES = 4


def _moba_post_kernel(h_ref, x_ref, wm_ref, g1_ref, b1_ref, win_ref, wout_ref, g2_ref, b2_ref, o_ref):
    sub = MOBA_POST_ROWS // MOBA_POST_SUBTILES

    def subtile(r):
        rows = slice(r * sub, (r + 1) * sub)
        mix = _dot(x_ref[rows, :], wm_ref[...])
        yield
        h1 = _layer_norm(ALPHA * h_ref[rows, :] + mix, g1_ref[...], b1_ref[...])
        yield from _ffn_stages(h1, win_ref, wout_ref, g2_ref, b2_ref, o_ref, rows)

    _run_pipelined([subtile(r) for r in range(MOBA_POST_SUBTILES)])


def _moba_post(h, x, wm, g1, b1, w_in, w_out, g2, b2):
    m, d = h.shape
    tm = MOBA_POST_ROWS
    return pl.pallas_call(
        _moba_post_kernel,
        grid=(m // tm,),
        in_specs=[pl.BlockSpec((tm, d), lambda i: (i, 0)),
                  pl.BlockSpec((tm, x.shape[1]), lambda i: (i, 0)),
                  _resident(wm.shape), _resident(g1.shape), _resident(b1.shape),
                  _resident(w_in.shape), _resident(w_out.shape), _resident(g2.shape), _resident(b2.shape)],
        out_specs=pl.BlockSpec((tm, d), lambda i: (i, 0)),
        out_shape=jax.ShapeDtypeStruct((m, d), F32),
        compiler_params=_params(("parallel",)),
        name="moba_post_ffn",
    )(h, x, wm, g1, b1, w_in, w_out, g2, b2)


GDN_POST_ROWS = 512
GDN_POST_SUBTILES = 2


def _gdn_post_kernel(h_ref, o_ref, wz_ref, nw_ref, wo_ref, g1_ref, b1_ref,
                     win_ref, wout_ref, g2_ref, b2_ref, out_ref):
    sub = GDN_POST_ROWS // GDN_POST_SUBTILES

    def subtile(r):
        rows = slice(r * sub, (r + 1) * sub)
        h = h_ref[rows, :]
        z = _dot(h.astype(BF16), wz_ref[...])
        yield
        nw = nw_ref[...]
        gated = []
        for n in range(GDN_V_HEADS):
            cs = slice(n * HEAD_DIM, (n + 1) * HEAD_DIM)
            o = o_ref[rows, cs].astype(F32)
            zz = z[:, cs]
            y = o * lax.rsqrt(jnp.mean(o * o, axis=-1, keepdims=True) + RMS_EPS) * nw * (zz * _sigmoid(zz))
            gated.append(y.astype(BF16))
        mix = _dot(jnp.concatenate(gated, axis=1), wo_ref[...])
        yield
        h1 = _layer_norm(ALPHA * h + mix, g1_ref[...], b1_ref[...])
        yield from _ffn_stages(h1, win_ref, wout_ref, g2_ref, b2_ref, out_ref, rows)

    _run_pipelined([subtile(r) for r in range(GDN_POST_SUBTILES)])


def _gdn_post(h, o, wz, nw, wo, g1, b1, w_in, w_out, g2, b2):
    m, d = h.shape
    tm = GDN_POST_ROWS
    return pl.pallas_call(
        _gdn_post_kernel,
        grid=(m // tm,),
        in_specs=[pl.BlockSpec((tm, d), lambda i: (i, 0)),
                  pl.BlockSpec((tm, GDN_V_DIM), lambda i: (i, 0)),
                  _resident(wz.shape), _resident(nw.shape), _resident(wo.shape),
                  _resident(g1.shape), _resident(b1.shape),
                  _resident(w_in.shape), _resident(w_out.shape), _resident(g2.shape), _resident(b2.shape)],
        out_specs=pl.BlockSpec((tm, d), lambda i: (i, 0)),
        out_shape=jax.ShapeDtypeStruct((m, d), F32),
        compiler_params=_params(("parallel",)),
        name="gdn_post_ffn",
    )(h, o, wz, nw, wo, g1, b1, w_in, w_out, g2, b2)


def _row(vec):
    return vec.reshape(1, -1).astype(F32)


def _pad_lanes(x2d):
    return jnp.pad(x2d, ((0, 0), (0, LANES - x2d.shape[1])))


def kernel(x, ln_g, ln_b, ffn_pre_w_in, ffn_pre_w_out, ffn_post_w_in, ffn_post_w_out,
           moba_w_in, moba_w_out, gdn_w_in, gdn_conv_w, gdn_a_log, gdn_dt_bias,
           gdn_norm_w, gdn_w_out):
    batch, seq, d = x.shape
    m = batch * seq
    h = x.reshape(m, d).astype(F32)
    cos, sin = _rope_tables(seq)

    for i in range(DEPTH):
        h = _ffn_block(h, ffn_pre_w_in[i].astype(BF16), ffn_pre_w_out[i].astype(BF16),
                       _row(ln_g[i, 0]), _row(ln_b[i, 0]))
        post = (ffn_post_w_in[i].astype(BF16), ffn_post_w_out[i].astype(BF16),
                _row(ln_g[i, 2]), _row(ln_b[i, 2]))
        j = i // 2
        if i % 2 == 0:
            w_in = moba_w_in[j]
            q, k, vt = _qkv_rope(h.reshape(batch, seq, d), w_in[:, :2 * D_MODEL].astype(BF16),
                                 w_in[:, 2 * D_MODEL:].T.astype(BF16), cos, sin)
            o = _moba_attention(q, k, vt)
            h = _moba_post(h, o.reshape(m, d), moba_w_out[j].astype(BF16),
                           _row(ln_g[i, 1]), _row(ln_b[i, 1]), *post)
        else:
            w_in = gdn_w_in[j]
            z0 = GDN_CONV_DIM
            b0 = z0 + GDN_V_DIM
            a0 = b0 + GDN_V_HEADS
            wqkv = w_in[:, :z0].astype(BF16)
            wz = w_in[:, z0:b0].astype(BF16)
            wb = _pad_lanes(w_in[:, b0:a0]).astype(BF16)
            wa = _pad_lanes(w_in[:, a0:a0 + GDN_V_HEADS]).astype(BF16)
            q, k, v, beta, gc = _gdn_in(
                h.reshape(batch, seq, d), wqkv, wb, wa, gdn_conv_w[j].astype(F32),
                _pad_lanes(_row(gdn_a_log[j])), _pad_lanes(_row(gdn_dt_bias[j])))
            o = _gdn_core(q, k, v, gc, beta)
            h = _gdn_post(h, o.reshape(m, GDN_V_DIM), wz, _row(gdn_norm_w[j]),
                          gdn_w_out[j].astype(BF16), _row(ln_g[i, 1]), _row(ln_b[i, 1]), *post)
    return h.reshape(batch, seq, d).astype(x.dtype)
```

```python
import functools
import math

import jax
import jax.numpy as jnp
from jax import lax
from jax.experimental import pallas as pl
from jax.experimental.pallas import tpu as pltpu

F32 = jnp.float32
BF16 = jnp.bfloat16

D_MODEL = 1024
DEPTH = 2
D_FF = 2816
HEAD_DIM = 128
MOBA_HEADS = D_MODEL // HEAD_DIM
MOBA_BLOCK = 256
MOBA_TOPK = 3
ROPE_THETA = 10000.0
GDN_QK_HEADS = D_MODEL // HEAD_DIM
GDN_V_HEADS = 2 * GDN_QK_HEADS
GDN_QK_DIM = GDN_QK_HEADS * HEAD_DIM
GDN_V_DIM = GDN_V_HEADS * HEAD_DIM
GDN_CONV_DIM = 2 * GDN_QK_DIM + GDN_V_DIM
GDN_CONV = 4
GDN_CHUNK = 64
ALPHA = (2 * DEPTH) ** 0.25
HALF_STEP = 0.5
LN_EPS = 1e-5
RMS_EPS = 1e-6
NEG_INF = -1e30

LANES = 128
SUBLANES = 8
VMEM_LIMIT = 56 * 1024 * 1024

NT_DIMS = (((1,), (1,)), ((), ()))
TN_DIMS = (((0,), (0,)), ((), ()))


def _dot(a, b):
    return jnp.dot(a, b, preferred_element_type=F32)


def _dot_nt(a, b):
    return lax.dot_general(a, b, NT_DIMS, preferred_element_type=F32)


def _sigmoid(x):
    return 1.0 / (1.0 + jnp.exp(-x))


def _layer_norm(y, g, b):
    mu = jnp.mean(y, axis=-1, keepdims=True)
    d = y - mu
    var = jnp.mean(d * d, axis=-1, keepdims=True)
    return d * lax.rsqrt(var + LN_EPS) * g + b


def _resident(shape):
    nd = len(shape)
    return pl.BlockSpec(shape, lambda *_: (0,) * nd, pipeline_mode=pl.Buffered(1))


def _params(semantics):
    return pltpu.CompilerParams(dimension_semantics=semantics, vmem_limit_bytes=VMEM_LIMIT)


FFN_ROWS = 1024
FFN_SUBTILES = 4


def _ffn_kernel(h_ref, win_ref, wout_ref, g_ref, b_ref, o_ref):
    sub = FFN_ROWS // FFN_SUBTILES
    for r in range(FFN_SUBTILES):
        rows = slice(r * sub, (r + 1) * sub)
        h = h_ref[rows, :]
        hb = h.astype(BF16)
        gate = _dot(hb, win_ref[:, :D_FF])
        up = _dot(hb, win_ref[:, D_FF:])
        act = (gate * _sigmoid(gate) * up).astype(BF16)
        y = ALPHA * h + HALF_STEP * _dot(act, wout_ref[...])
        o_ref[rows, :] = _layer_norm(y, g_ref[...], b_ref[...])


def _ffn_block(h, w_in, w_out, g, b):
    m, d = h.shape
    return pl.pallas_call(
        _ffn_kernel,
        grid=(m // FFN_ROWS,),
        in_specs=[
            pl.BlockSpec((FFN_ROWS, d), lambda i: (i, 0)),
            _resident(w_in.shape),
            _resident(w_out.shape),
            _resident(g.shape),
            _resident(b.shape),
        ],
        out_specs=pl.BlockSpec((FFN_ROWS, d), lambda i: (i, 0)),
        out_shape=jax.ShapeDtypeStruct((m, d), F32),
        compiler_params=_params(("parallel",)),
        name="ffn_ln",
    )(h, w_in, w_out, g, b)


QKV_ROWS = 512
BF16_SUBLANES = 16
MOBA_VT_ROWS = HEAD_DIM + BF16_SUBLANES
MOBA_EXP2_SCALE = (HEAD_DIM ** -0.5) * math.log2(math.e)


def _qkv_rope_kernel(h_ref, wqk_ref, wvt_ref, cos_ref, sin_ref, q_ref, k_ref, vt_ref):
    hb = h_ref[...].astype(BF16)
    cos = cos_ref[...]
    sin = sin_ref[...]
    for part, out_ref, gain in ((0, q_ref, MOBA_EXP2_SCALE), (1, k_ref, 1.0)):
        x = _dot(hb, wqk_ref[:, part * D_MODEL:(part + 1) * D_MODEL])
        for hh in range(MOBA_HEADS):
            xs = x[:, hh * HEAD_DIM:(hh + 1) * HEAD_DIM]
            rot = pltpu.roll(xs, HEAD_DIM // 2, axis=1)
            y = xs * cos + rot * sin
            if gain != 1.0:
                y = y * gain
            out_ref[:, hh * HEAD_DIM:(hh + 1) * HEAD_DIM] = y.astype(BF16)
    vt = _dot_nt(wvt_ref[...], hb).astype(BF16)
    ones = jnp.ones((MOBA_VT_ROWS - HEAD_DIM, MOBA_BLOCK), BF16)
    for hh in range(MOBA_HEADS):
        for blk in range(QKV_ROWS // MOBA_BLOCK):
            vt_ref[hh, blk, 0:HEAD_DIM, :] = vt[hh * HEAD_DIM:(hh + 1) * HEAD_DIM,
                                                blk * MOBA_BLOCK:(blk + 1) * MOBA_BLOCK]
            vt_ref[hh, blk, HEAD_DIM:MOBA_VT_ROWS, :] = ones


def _rope_tables(seq):
    half = HEAD_DIM // 2
    inv_freq = ROPE_THETA ** (-jnp.arange(half, dtype=F32) / half)
    ang = jnp.arange(seq).astype(F32)[:, None] * inv_freq[None, :]
    cos = jnp.cos(ang)
    sin = jnp.sin(ang)
    return jnp.concatenate([cos, cos], axis=-1), jnp.concatenate([-sin, sin], axis=-1)


def _qkv_rope(h3, wqk, wvt, cos, sin):
    batch, seq, d = h3.shape
    tm = QKV_ROWS
    blocks = tm // MOBA_BLOCK
    row_spec = pl.BlockSpec((None, tm, d), lambda b, t: (b, t, 0))
    tab_spec = pl.BlockSpec((tm, HEAD_DIM), lambda b, t: (t, 0))
    vt_spec = pl.BlockSpec((None, MOBA_HEADS, blocks, MOBA_VT_ROWS, MOBA_BLOCK), lambda b, t: (b, 0, t, 0, 0))
    qk_out = jax.ShapeDtypeStruct((batch, seq, d), BF16)
    vt_out = jax.ShapeDtypeStruct((batch, MOBA_HEADS, seq // MOBA_BLOCK, MOBA_VT_ROWS, MOBA_BLOCK), BF16)
    return pl.pallas_call(
        _qkv_rope_kernel,
        grid=(batch, seq // tm),
        in_specs=[row_spec, _resident(wqk.shape), _resident(wvt.shape), tab_spec, tab_spec],
        out_specs=[row_spec, row_spec, vt_spec],
        out_shape=[qk_out, qk_out, vt_out],
        compiler_params=_params(("parallel", "parallel")),
        name="moba_qkv_rope",
    )(h3, wqk, wvt, cos, sin)


MOBA_TILES_PER_STEP = 16


def _moba_attn_kernel(q_ref, k_ref, vt_ref, o_ref, kmean_ref, *, n_blk, n_pad, per_step):
    step = pl.program_id(2)
    bs = MOBA_BLOCK

    @pl.when(step == 0)
    def _():
        kmean_ref[...] = jnp.zeros_like(kmean_ref)
        for n in range(n_blk):
            kb = k_ref[n * bs:(n + 1) * bs, :].astype(F32)
            kmean_ref[n:n + 1, :] = jnp.sum(kb, axis=0, keepdims=True) * (1.0 / bs)

    def tile(n_past):
        q = q_ref[n_past * bs:(n_past + 1) * bs, :]
        if n_past > 0:
            gate = _dot_nt(kmean_ref[...].astype(BF16), q)
        s = _dot_nt(k_ref[n_past * bs:(n_past + 1) * bs, :], q)
        if n_past > 0:
            s_past = _dot_nt(k_ref[0:n_past * bs, :], q)
        yield

        key = lax.broadcasted_iota(jnp.int32, s.shape, 0)
        qry = lax.broadcasted_iota(jnp.int32, s.shape, 1)
        s = jnp.where(key <= qry, s, NEG_INF)
        m = jnp.max(s, axis=0, keepdims=True)
        p = jnp.exp2(s - m)
        acc = _dot(vt_ref[n_past], p.astype(BF16))
        yield

        if n_past > 0:
            blk_f = lax.broadcasted_iota(jnp.int32, gate.shape, 0).astype(F32)
            past = blk_f < float(n_past)
            gate = jnp.where(past, gate, NEG_INF)
            sel = jnp.zeros(gate.shape, F32)
            for _ in range(min(MOBA_TOPK, n_blk)):
                top = jnp.max(gate, axis=0, keepdims=True)
                first = jnp.min(jnp.where(gate == top, blk_f, float(n_pad)), axis=0, keepdims=True)
                pick = blk_f == first
                sel = jnp.where(jnp.logical_and(pick, past), 1.0, sel)
                gate = jnp.where(pick, -jnp.inf, gate)

            for n in range(n_past):
                chosen = sel[n:n + 1, :] > 0.5
                s = s_past[n * bs:(n + 1) * bs]
                m_blk = jnp.where(chosen, jnp.max(s, axis=0, keepdims=True), NEG_INF)
                m_new = jnp.maximum(m, m_blk)
                corr = jnp.exp2(m - m_new)
                p = jnp.exp2(s - jnp.where(chosen, m_new, jnp.inf))
                acc = acc * corr + _dot(vt_ref[n], p.astype(BF16))
                m = m_new
                yield

        o = acc[0:HEAD_DIM] / acc[HEAD_DIM:HEAD_DIM + 1]
        o_ref[n_past * bs:(n_past + 1) * bs, :] = o.T.astype(o_ref.dtype)

    def tile_group(i):
        half = per_step // 2
        per_half = n_blk // per_step
        tiles = []
        for g in range(half):
            lo = g * 2 * per_half + i
            tiles += [n_blk - 1 - lo, lo]
        progs = [tile(n) for n in sorted(tiles, reverse=True)]
        while progs:
            progs = [prog for prog in progs if next(prog, "done") != "done"]

    for i in range(n_blk // per_step):
        pl.when(step == i)(functools.partial(tile_group, i))


def _moba_attention(q, k, vt):
    batch, seq, _ = q.shape
    n_blk = seq // MOBA_BLOCK
    n_pad = -(-n_blk // SUBLANES) * SUBLANES
    per_step = min(MOBA_TILES_PER_STEP, n_blk)
    assert seq % MOBA_BLOCK == 0 and per_step % 2 == 0 and n_blk % per_step == 0
    q_spec = pl.BlockSpec((None, seq, HEAD_DIM), lambda b, h, i: (b, 0, h))
    k_spec = q_spec
    vt_spec = pl.BlockSpec((None, None, n_blk, MOBA_VT_ROWS, MOBA_BLOCK), lambda b, h, i: (b, h, 0, 0, 0))
    return pl.pallas_call(
        functools.partial(_moba_attn_kernel, n_blk=n_blk, n_pad=n_pad, per_step=per_step),
        grid=(batch, MOBA_HEADS, n_blk // per_step),
        in_specs=[q_spec, k_spec, vt_spec],
        out_specs=q_spec,
        out_shape=jax.ShapeDtypeStruct(q.shape, BF16),
        scratch_shapes=[
            pltpu.VMEM((n_pad, HEAD_DIM), F32),
        ],
        compiler_params=_params(("parallel", "parallel", "arbitrary")),
        name="moba_attn",
    )(q, k, vt)


GDN_IN_ROWS = 256


def _gdn_in_kernel(h_ref, wqkv_ref, wb_ref, wa_ref, cw_ref, alog_ref, dtb_ref,
                   q_ref, k_ref, v_ref, beta_ref, gc_ref, xs_ref):
    tm = GDN_IN_ROWS
    halo = SUBLANES
    hb = h_ref[...].astype(BF16)

    @pl.when(pl.program_id(1) == 0)
    def _():
        xs_ref[0:halo, :] = jnp.zeros((halo, GDN_CONV_DIM), F32)

    xs_ref[halo:halo + tm, :] = _dot(hb, wqkv_ref[...])

    n_cols = GDN_CONV_DIM // HEAD_DIM
    for c in range(n_cols):
        cs = slice(c * HEAD_DIM, (c + 1) * HEAD_DIM)
        xe = xs_ref[:, cs]
        y = xe[halo:] * cw_ref[GDN_CONV - 1:GDN_CONV, cs]
        for back in range(1, GDN_CONV):
            i = GDN_CONV - 1 - back
            y = y + pltpu.roll(xe, back, axis=0)[halo:] * cw_ref[i:i + 1, cs]
        y = y * _sigmoid(y)
        if c < 2 * GDN_QK_HEADS:
            y = y * lax.rsqrt(jnp.sum(y * y, axis=-1, keepdims=True) + RMS_EPS)
            if c < GDN_QK_HEADS:
                q_ref[c] = (y * (HEAD_DIM ** -0.5)).astype(BF16)
            else:
                k_ref[c - GDN_QK_HEADS] = y.astype(BF16)
        else:
            v_ref[c - 2 * GDN_QK_HEADS] = y.astype(BF16)

    xs_ref[0:halo, :] = xs_ref[tm:tm + halo, :]

    beta = _sigmoid(_dot(hb, wb_ref[...]))
    a = _dot(hb, wa_ref[...]) + dtb_ref[...]
    softplus = jnp.maximum(a, 0.0) + jnp.log1p(jnp.exp(-jnp.abs(a)))
    g = -jnp.exp(alog_ref[...]) * softplus
    ri = lax.broadcasted_iota(jnp.int32, (tm, tm), 0)
    ci = lax.broadcasted_iota(jnp.int32, (tm, tm), 1)
    tri = jnp.logical_and(ri >= ci, ri // GDN_CHUNK == ci // GDN_CHUNK).astype(F32)
    gc = jnp.dot(tri, g, preferred_element_type=F32, precision=lax.Precision.HIGHEST)
    beta_ref[...] = beta[:, :GDN_V_HEADS]
    gc_ref[...] = gc[:, :GDN_V_HEADS]


def _gdn_in(h3, wqkv, wb, wa, conv_w, alog, dtb):
    batch, seq, d = h3.shape
    tm = GDN_IN_ROWS

    def rows(width):
        return pl.BlockSpec((None, tm, width), lambda b, t: (b, t, 0))

    def heads(n):
        return pl.BlockSpec((None, n, tm, HEAD_DIM), lambda b, t: (b, 0, t, 0))

    def head_out(n):
        return jax.ShapeDtypeStruct((batch, n, seq, HEAD_DIM), BF16)

    gate_out = jax.ShapeDtypeStruct((batch, seq, GDN_V_HEADS), F32)
    return pl.pallas_call(
        _gdn_in_kernel,
        grid=(batch, seq // tm),
        in_specs=[rows(d), _resident(wqkv.shape), _resident(wb.shape), _resident(wa.shape),
                  _resident(conv_w.shape), _resident(alog.shape), _resident(dtb.shape)],
        out_specs=[heads(GDN_QK_HEADS), heads(GDN_QK_HEADS), heads(GDN_V_HEADS),
                   rows(GDN_V_HEADS), rows(GDN_V_HEADS)],
        out_shape=[head_out(GDN_QK_HEADS), head_out(GDN_QK_HEADS), head_out(GDN_V_HEADS),
                   gate_out, gate_out],
        scratch_shapes=[pltpu.VMEM((tm + SUBLANES, GDN_CONV_DIM), F32)],
        compiler_params=_params(("parallel", "arbitrary")),
        name="gdn_in",
    )(h3, wqkv, wb, wa, conv_w, alog, dtb)


GDN_TILE = 256


def _gdn_core_kernel(q_ref, k_ref, v_ref, gcol_ref, bcol_ref, grow_ref, o_ref,
                     s_ref, u_ref, wq_ref, kd_ref, attn_ref, gl_ref, bd_ref):
    tile = GDN_TILE
    c_len = GDN_CHUNK
    n_chunks = tile // c_len
    group = GDN_V_HEADS // GDN_QK_HEADS

    @pl.when(pl.program_id(1) == 0)
    def _():
        s_ref[...] = jnp.zeros_like(s_ref)

    ii = lax.broadcasted_iota(jnp.int32, (tile, tile), 0)
    jj = lax.broadcasted_iota(jnp.int32, (tile, tile), 1)
    bd_ref[...] = ((ii // c_len) == (jj // c_len)).astype(BF16)
    ci = lax.broadcasted_iota(jnp.int32, (c_len, tile), 0)
    cj = lax.broadcasted_iota(jnp.int32, (c_len, tile), 1)
    cj_in = cj % c_len
    cj_chunk = cj // c_len
    cat_lower = ci >= cj_in
    cat_strict = ci > cj_in
    eye_cat = (ci == cj_in).astype(F32)
    half_lane = lax.broadcasted_iota(jnp.int32, (c_len, LANES), 1) < c_len
    head_lane = lax.broadcasted_iota(jnp.int32, (tile, GDN_V_HEADS), 1)

    def diag_blocks(x_full):
        out = x_full[0:c_len]
        for c in range(1, n_chunks):
            out = jnp.where(cj_chunk == c, x_full[c * c_len:(c + 1) * c_len], out)
        return out

    def rows_to_cat(x_b):
        halves = []
        for c in range(0, n_chunks, 2):
            halves.append(jnp.where(half_lane, x_b[c * c_len:(c + 1) * c_len],
                                    x_b[(c + 1) * c_len:(c + 2) * c_len]))
        return jnp.concatenate(halves, axis=1)

    def expand(x_cat):
        return jnp.concatenate([x_cat] * n_chunks, axis=0) * bd_ref[...]

    def prepare(hk, q_t, k_t, kk_cat, qk_cat, j):
        hv = hk * group + j
        pick = head_lane == hv
        gcc = jnp.sum(jnp.where(pick, gcol_ref[...], 0.0), axis=-1, keepdims=True)
        beta = jnp.sum(jnp.where(pick, bcol_ref[...], 0.0), axis=-1, keepdims=True)
        gcc_b = jnp.broadcast_to(gcc, (tile, LANES))
        beta_b = jnp.broadcast_to(beta, (tile, LANES))
        gcr = grow_ref[hv]
        decay = jnp.exp(jnp.where(cat_lower, rows_to_cat(gcc_b) - gcr, NEG_INF))
        attn_cat = (qk_cat * decay).astype(BF16)
        for c in range(n_chunks):
            attn_ref[hv, c] = attn_cat[:, c * c_len:(c + 1) * c_len]
        p_cat = jnp.where(cat_strict, -(rows_to_cat(beta_b) * kk_cat * decay), 0.0)
        gam_b = jnp.exp(gcc_b)
        k32 = k_t.astype(F32)
        v32 = v_ref[hv].astype(F32)
        rhs = jnp.concatenate([v32 * beta_b, k32 * (beta_b * gam_b)], axis=1).astype(BF16)
        q_dec = (q_t.astype(F32) * gam_b).astype(BF16)
        g_last_b = jnp.concatenate(
            [jnp.broadcast_to(gcc_b[(c + 1) * c_len - 1:(c + 1) * c_len], (c_len, LANES))
             for c in range(n_chunks)], axis=0)
        kd_ref[hv] = (k32 * jnp.exp(g_last_b - gcc_b)).astype(BF16)
        for c in range(n_chunks):
            wq_ref[hv, (2 * c + 1) * c_len:(2 * c + 2) * c_len] = q_dec[c * c_len:(c + 1) * c_len]
            gl_ref[hv, c] = jnp.exp(g_last_b[c * c_len:c * c_len + SUBLANES])
        return dict(hv=hv, p_bd=expand(p_cat.astype(BF16)), p_cat=p_cat, t_cat=eye_cat + p_cat, rhs=rhs)

    def head_program(hk, j, shared):
        if not shared:
            q_t = q_ref[hk]
            k_t = k_ref[hk]
            shared.update(q_t=q_t, k_t=k_t, kk_cat=diag_blocks(_dot_nt(k_t, k_t)),
                          qk_cat=diag_blocks(_dot_nt(q_t, k_t)))
        h = prepare(hk, shared["q_t"], shared["k_t"], shared["kk_cat"], shared["qk_cat"], j)
        yield
        p_cat = _dot(h["p_cat"].astype(BF16), h["p_bd"])
        t_cat = h["t_cat"]
        yield
        n = 2
        while 2 * n < c_len:
            both = _dot(jnp.concatenate([t_cat, p_cat], axis=0).astype(BF16), expand(p_cat.astype(BF16)))
            t_cat = t_cat + both[:c_len]
            p_cat = both[c_len:]
            n *= 2
            yield
        t_cat = t_cat + _dot(t_cat.astype(BF16), expand(p_cat.astype(BF16)))
        yield
        uw = _dot(expand(t_cat.astype(BF16)), h["rhs"])
        hv = h["hv"]
        u_ref[hv] = uw[:, :HEAD_DIM]
        w_b = uw[:, HEAD_DIM:].astype(BF16)
        for c in range(n_chunks):
            wq_ref[hv, 2 * c * c_len:(2 * c + 1) * c_len] = w_b[c * c_len:(c + 1) * c_len]

    progs = []
    for hk in range(GDN_QK_HEADS):
        shared = {}
        for j in range(group):
            progs.append(head_program(hk, j, shared))
    while progs:
        progs = [prog for prog in progs if next(prog, "done") != "done"]

    for c in range(n_chunks):
        rows = slice(c * c_len, (c + 1) * c_len)
        boths = [_dot(wq_ref[hv, 2 * c * c_len:(2 * c + 2) * c_len, :], s_ref[hv].astype(BF16))
                 for hv in range(GDN_V_HEADS)]
        for hv in range(GDN_V_HEADS):
            v_new = (u_ref[hv, rows, :] - boths[hv][:c_len]).astype(BF16)
            o_ref[rows, hv * HEAD_DIM:(hv + 1) * HEAD_DIM] = (
                boths[hv][c_len:] + _dot(attn_ref[hv, c], v_new)).astype(o_ref.dtype)
            s_ref[hv] = s_ref[hv] * gl_ref[hv, c][0:1, :] + lax.dot_general(
                kd_ref[hv, rows, :], v_new, TN_DIMS, preferred_element_type=F32)


def _gdn_core(q, k, v, gc, beta):
    batch, _, seq, _ = q.shape
    tile = GDN_TILE
    n_chunks = tile // GDN_CHUNK
    nh = GDN_V_HEADS
    g_row = gc.transpose(0, 2, 1).reshape(batch, nh, seq // tile, 1, tile)

    def heads(n):
        return pl.BlockSpec((None, n, tile, HEAD_DIM), lambda b, t: (b, 0, t, 0))

    col_spec = pl.BlockSpec((None, tile, nh), lambda b, t: (b, t, 0))
    row_spec = pl.BlockSpec((None, nh, None, 1, tile), lambda b, t: (b, 0, t, 0, 0))
    return pl.pallas_call(
        _gdn_core_kernel,
        grid=(batch, seq // tile),
        in_specs=[heads(GDN_QK_HEADS), heads(GDN_QK_HEADS), heads(nh), col_spec, col_spec, row_spec],
        out_specs=pl.BlockSpec((None, tile, GDN_V_DIM), lambda b, t: (b, t, 0)),
        out_shape=jax.ShapeDtypeStruct((batch, seq, GDN_V_DIM), BF16),
        scratch_shapes=[
            pltpu.VMEM((nh, HEAD_DIM, HEAD_DIM), F32),
            pltpu.VMEM((nh, tile, HEAD_DIM), F32),
            pltpu.VMEM((nh, 2 * tile, HEAD_DIM), BF16),
            pltpu.VMEM((nh, tile, HEAD_DIM), BF16),
            pltpu.VMEM((nh, n_chunks, GDN_CHUNK, GDN_CHUNK), BF16),
            pltpu.VMEM((nh, n_chunks, SUBLANES, LANES), F32),
            pltpu.VMEM((tile, tile), BF16),
        ],
        compiler_params=_params(("parallel", "arbitrary")),
        name="gdn_core",
    )(q, k, v, gc, beta, g_row)


def _run_pipelined(progs):
    pending = list(progs)
    active = []
    while pending or active:
        if pending:
            active.append(pending.pop(0))
        active = [prog for prog in active if next(prog, "done") != "done"]


def _ffn_stages(h1, win_ref, wout_ref, g_ref, b_ref, o_ref, rows):
    hb = h1.astype(BF16)
    gate = _dot(hb, win_ref[:, :D_FF])
    up = _dot(hb, win_ref[:, D_FF:])
    yield
    act = (gate * _sigmoid(gate) * up).astype(BF16)
    y = ALPHA * h1 + HALF_STEP * _dot(act, wout_ref[...])
    yield
    o_ref[rows, :] = _layer_norm(y, g_ref[...], b_ref[...])


MOBA_POST_ROWS = 1024
MOBA_POST_SUBTILES = 4


def _moba_post_kernel(h_ref, x_ref, wm_ref, g1_ref, b1_ref, win_ref, wout_ref, g2_ref, b2_ref, o_ref):
    sub = MOBA_POST_ROWS // MOBA_POST_SUBTILES

    def subtile(r):
        rows = slice(r * sub, (r + 1) * sub)
        mix = _dot(x_ref[rows, :], wm_ref[...])
        yield
        h1 = _layer_norm(ALPHA * h_ref[rows, :] + mix, g1_ref[...], b1_ref[...])
        yield from _ffn_stages(h1, win_ref, wout_ref, g2_ref, b2_ref, o_ref, rows)

    _run_pipelined([subtile(r) for r in range(MOBA_POST_SUBTILES)])


def _moba_post(h, x, wm, g1, b1, w_in, w_out, g2, b2):
    m, d = h.shape
    tm = MOBA_POST_ROWS
    return pl.pallas_call(
        _moba_post_kernel,
        grid=(m // tm,),
        in_specs=[pl.BlockSpec((tm, d), lambda i: (i, 0)),
                  pl.BlockSpec((tm, x.shape[1]), lambda i: (i, 0)),
                  _resident(wm.shape), _resident(g1.shape), _resident(b1.shape),
                  _resident(w_in.shape), _resident(w_out.shape), _resident(g2.shape), _resident(b2.shape)],
        out_specs=pl.BlockSpec((tm, d), lambda i: (i, 0)),
        out_shape=jax.ShapeDtypeStruct((m, d), F32),
        compiler_params=_params(("parallel",)),
        name="moba_post_ffn",
    )(h, x, wm, g1, b1, w_in, w_out, g2, b2)


GDN_POST_ROWS = 512
GDN_POST_SUBTILES = 2


def _gdn_post_kernel(h_ref, o_ref, wz_ref, nw_ref, wo_ref, g1_ref, b1_ref,
                     win_ref, wout_ref, g2_ref, b2_ref, out_ref):
    sub = GDN_POST_ROWS // GDN_POST_SUBTILES

    def subtile(r):
        rows = slice(r * sub, (r + 1) * sub)
        h = h_ref[rows, :]
        z = _dot(h.astype(BF16), wz_ref[...])
        yield
        nw = nw_ref[...]
        gated = []
        for n in range(GDN_V_HEADS):
            cs = slice(n * HEAD_DIM, (n + 1) * HEAD_DIM)
            o = o_ref[rows, cs].astype(F32)
            zz = z[:, cs]
            y = o * lax.rsqrt(jnp.mean(o * o, axis=-1, keepdims=True) + RMS_EPS) * nw * (zz * _sigmoid(zz))
            gated.append(y.astype(BF16))
        mix = _dot(jnp.concatenate(gated, axis=1), wo_ref[...])
        yield
        h1 = _layer_norm(ALPHA * h + mix, g1_ref[...], b1_ref[...])
        yield from _ffn_stages(h1, win_ref, wout_ref, g2_ref, b2_ref, out_ref, rows)

    _run_pipelined([subtile(r) for r in range(GDN_POST_SUBTILES)])


def _gdn_post(h, o, wz, nw, wo, g1, b1, w_in, w_out, g2, b2):
    m, d = h.shape
    tm = GDN_POST_ROWS
    return pl.pallas_call(
        _gdn_post_kernel,
        grid=(m // tm,),
        in_specs=[pl.BlockSpec((tm, d), lambda i: (i, 0)),
                  pl.BlockSpec((tm, GDN_V_DIM), lambda i: (i, 0)),
                  _resident(wz.shape), _resident(nw.shape), _resident(wo.shape),
                  _resident(g1.shape), _resident(b1.shape),
                  _resident(w_in.shape), _resident(w_out.shape), _resident(g2.shape), _resident(b2.shape)],
        out_specs=pl.BlockSpec((tm, d), lambda i: (i, 0)),
        out_shape=jax.ShapeDtypeStruct((m, d), F32),
        compiler_params=_params(("parallel",)),
        name="gdn_post_ffn",
    )(h, o, wz, nw, wo, g1, b1, w_in, w_out, g2, b2)


def _row(vec):
    return vec.reshape(1, -1).astype(F32)


def _pad_lanes(x2d):
    return jnp.pad(x2d, ((0, 0), (0, LANES - x2d.shape[1])))


def kernel(x, ln_g, ln_b, ffn_pre_w_in, ffn_pre_w_out, ffn_post_w_in, ffn_post_w_out,
           moba_w_in, moba_w_out, gdn_w_in, gdn_conv_w, gdn_a_log, gdn_dt_bias,
           gdn_norm_w, gdn_w_out):
    batch, seq, d = x.shape
    m = batch * seq
    h = x.reshape(m, d).astype(F32)
    cos, sin = _rope_tables(seq)

    for i in range(DEPTH):
        h = _ffn_block(h, ffn_pre_w_in[i].astype(BF16), ffn_pre_w_out[i].astype(BF16),
                       _row(ln_g[i, 0]), _row(ln_b[i, 0]))
        post = (ffn_post_w_in[i].astype(BF16), ffn_post_w_out[i].astype(BF16),
                _row(ln_g[i, 2]), _row(ln_b[i, 2]))
        j = i // 2
        if i % 2 == 0:
            w_in = moba_w_in[j]
            q, k, vt = _qkv_rope(h.reshape(batch, seq, d), w_in[:, :2 * D_MODEL].astype(BF16),
                                 w_in[:, 2 * D_MODEL:].T.astype(BF16), cos, sin)
            o = _moba_attention(q, k, vt)
            h = _moba_post(h, o.reshape(m, d), moba_w_out[j].astype(BF16),
                           _row(ln_g[i, 1]), _row(ln_b[i, 1]), *post)
        else:
            w_in = gdn_w_in[j]
            z0 = GDN_CONV_DIM
            b0 = z0 + GDN_V_DIM
            a0 = b0 + GDN_V_HEADS
            wqkv = w_in[:, :z0].astype(BF16)
            wz = w_in[:, z0:b0].astype(BF16)
            wb = _pad_lanes(w_in[:, b0:a0]).astype(BF16)
            wa = _pad_lanes(w_in[:, a0:a0 + GDN_V_HEADS]).astype(BF16)
            q, k, v, beta, gc = _gdn_in(
                h.reshape(batch, seq, d), wqkv, wb, wa, gdn_conv_w[j].astype(F32),
                _pad_lanes(_row(gdn_a_log[j])), _pad_lanes(_row(gdn_dt_bias[j])))
            o = _gdn_core(q, k, v, gc, beta)
            h = _gdn_post(h, o.reshape(m, GDN_V_DIM), wz, _row(gdn_norm_w[j]),
                          gdn_w_out[j].astype(BF16), _row(ln_g[i, 1]), _row(ln_b[i, 1]), *post)
    return h.reshape(batch, seq, d).astype(x.dtype)
```

```python
import functools
import math

import jax
import jax.numpy as jnp
from jax import lax
from jax.experimental import pallas as pl
from jax.experimental.pallas import tpu as pltpu

F32 = jnp.float32
BF16 = jnp.bfloat16

D_MODEL = 1024
DEPTH = 2
D_FF = 2816
HEAD_DIM = 128
MOBA_HEADS = D_MODEL // HEAD_DIM
MOBA_BLOCK = 256
MOBA_TOPK = 3
ROPE_THETA = 10000.0
GDN_QK_HEADS = D_MODEL // HEAD_DIM
GDN_V_HEADS = 2 * GDN_QK_HEADS
GDN_QK_DIM = GDN_QK_HEADS * HEAD_DIM
GDN_V_DIM = GDN_V_HEADS * HEAD_DIM
GDN_CONV_DIM = 2 * GDN_QK_DIM + GDN_V_DIM
GDN_CONV = 4
GDN_CHUNK = 64
ALPHA = (2 * DEPTH) ** 0.25
HALF_STEP = 0.5
LN_EPS = 1e-5
RMS_EPS = 1e-6
NEG_INF = -1e30

LANES = 128
SUBLANES = 8
VMEM_LIMIT = 56 * 1024 * 1024

NT_DIMS = (((1,), (1,)), ((), ()))
TN_DIMS = (((0,), (0,)), ((), ()))


def _dot(a, b):
    return jnp.dot(a, b, preferred_element_type=F32)


def _dot_nt(a, b):
    return lax.dot_general(a, b, NT_DIMS, preferred_element_type=F32)


def _sigmoid(x):
    return 1.0 / (1.0 + jnp.exp(-x))


def _layer_norm(y, g, b):
    mu = jnp.mean(y, axis=-1, keepdims=True)
    d = y - mu
    var = jnp.mean(d * d, axis=-1, keepdims=True)
    return d * lax.rsqrt(var + LN_EPS) * g + b


def _resident(shape):
    nd = len(shape)
    return pl.BlockSpec(shape, lambda *_: (0,) * nd, pipeline_mode=pl.Buffered(1))


def _params(semantics):
    return pltpu.CompilerParams(dimension_semantics=semantics, vmem_limit_bytes=VMEM_LIMIT)


FFN_ROWS = 1024
FFN_SUBTILES = 4


def _ffn_kernel(h_ref, win_ref, wout_ref, g_ref, b_ref, o_ref):
    sub = FFN_ROWS // FFN_SUBTILES
    for r in range(FFN_SUBTILES):
        rows = slice(r * sub, (r + 1) * sub)
        h = h_ref[rows, :]
        hb = h.astype(BF16)
        gate = _dot(hb, win_ref[:, :D_FF])
        up = _dot(hb, win_ref[:, D_FF:])
        act = (gate * _sigmoid(gate) * up).astype(BF16)
        y = ALPHA * h + HALF_STEP * _dot(act, wout_ref[...])
        o_ref[rows, :] = _layer_norm(y, g_ref[...], b_ref[...])


def _ffn_block(h, w_in, w_out, g, b):
    m, d = h.shape
    return pl.pallas_call(
        _ffn_kernel,
        grid=(m // FFN_ROWS,),
        in_specs=[
            pl.BlockSpec((FFN_ROWS, d), lambda i: (i, 0)),
            _resident(w_in.shape),
            _resident(w_out.shape),
            _resident(g.shape),
            _resident(b.shape),
        ],
        out_specs=pl.BlockSpec((FFN_ROWS, d), lambda i: (i, 0)),
        out_shape=jax.ShapeDtypeStruct((m, d), F32),
        compiler_params=_params(("parallel",)),
        name="ffn_ln",
    )(h, w_in, w_out, g, b)


QKV_ROWS = 512
BF16_SUBLANES = 16
MOBA_VT_ROWS = HEAD_DIM + BF16_SUBLANES
MOBA_EXP2_SCALE = (HEAD_DIM ** -0.5) * math.log2(math.e)


def _qkv_rope_kernel(h_ref, wqk_ref, wvt_ref, cos_ref, sin_ref, q_ref, k_ref, vt_ref):
    hb = h_ref[...].astype(BF16)
    cos = cos_ref[...]
    sin = sin_ref[...]
    for part, out_ref, gain in ((0, q_ref, MOBA_EXP2_SCALE), (1, k_ref, 1.0)):
        x = _dot(hb, wqk_ref[:, part * D_MODEL:(part + 1) * D_MODEL])
        for hh in range(MOBA_HEADS):
            xs = x[:, hh * HEAD_DIM:(hh + 1) * HEAD_DIM]
            rot = pltpu.roll(xs, HEAD_DIM // 2, axis=1)
            y = xs * cos + rot * sin
            if gain != 1.0:
                y = y * gain
            out_ref[:, hh * HEAD_DIM:(hh + 1) * HEAD_DIM] = y.astype(BF16)
    vt = _dot_nt(wvt_ref[...], hb).astype(BF16)
    ones = jnp.ones((MOBA_VT_ROWS - HEAD_DIM, MOBA_BLOCK), BF16)
    for hh in range(MOBA_HEADS):
        for blk in range(QKV_ROWS // MOBA_BLOCK):
            vt_ref[hh, blk, 0:HEAD_DIM, :] = vt[hh * HEAD_DIM:(hh + 1) * HEAD_DIM,
                                                blk * MOBA_BLOCK:(blk + 1) * MOBA_BLOCK]
            vt_ref[hh, blk, HEAD_DIM:MOBA_VT_ROWS, :] = ones


def _rope_tables(seq):
    half = HEAD_DIM // 2
    inv_freq = ROPE_THETA ** (-jnp.arange(half, dtype=F32) / half)
    ang = jnp.arange(seq).astype(F32)[:, None] * inv_freq[None, :]
    cos = jnp.cos(ang)
    sin = jnp.sin(ang)
    return jnp.concatenate([cos, cos], axis=-1), jnp.concatenate([-sin, sin], axis=-1)


def _qkv_rope(h3, wqk, wvt, cos, sin):
    batch, seq, d = h3.shape
    tm = QKV_ROWS
    blocks = tm // MOBA_BLOCK
    row_spec = pl.BlockSpec((None, tm, d), lambda b, t: (b, t, 0))
    tab_spec = pl.BlockSpec((tm, HEAD_DIM), lambda b, t: (t, 0))
    vt_spec = pl.BlockSpec((None, MOBA_HEADS, blocks, MOBA_VT_ROWS, MOBA_BLOCK), lambda b, t: (b, 0, t, 0, 0))
    qk_out = jax.ShapeDtypeStruct((batch, seq, d), BF16)
    vt_out = jax.ShapeDtypeStruct((batch, MOBA_HEADS, seq // MOBA_BLOCK, MOBA_VT_ROWS, MOBA_BLOCK), BF16)
    return pl.pallas_call(
        _qkv_rope_kernel,
        grid=(batch, seq // tm),
        in_specs=[row_spec, _resident(wqk.shape), _resident(wvt.shape), tab_spec, tab_spec],
        out_specs=[row_spec, row_spec, vt_spec],
        out_shape=[qk_out, qk_out, vt_out],
        compiler_params=_params(("parallel", "parallel")),
        name="moba_qkv_rope",
    )(h3, wqk, wvt, cos, sin)


MOBA_TILES_PER_STEP = 16


def _moba_attn_kernel(q_ref, k_ref, vt_ref, o_ref, kmean_ref, *, n_blk, n_pad, per_step):
    step = pl.program_id(2)
    bs = MOBA_BLOCK

    @pl.when(step == 0)
    def _():
        kmean_ref[...] = jnp.zeros_like(kmean_ref)
        for n in range(n_blk):
            kb = k_ref[n * bs:(n + 1) * bs, :].astype(F32)
            kmean_ref[n:n + 1, :] = jnp.sum(kb, axis=0, keepdims=True) * (1.0 / bs)

    def tile(n_past):
        q = q_ref[n_past * bs:(n_past + 1) * bs, :]
        if n_past > 0:
            gate = _dot_nt(kmean_ref[...].astype(BF16), q)
        s = _dot_nt(k_ref[n_past * bs:(n_past + 1) * bs, :], q)
        if n_past > 0:
            s_past = _dot_nt(k_ref[0:n_past * bs, :], q)
        yield

        key = lax.broadcasted_iota(jnp.int32, s.shape, 0)
        qry = lax.broadcasted_iota(jnp.int32, s.shape, 1)
        s = jnp.where(key <= qry, s, NEG_INF)
        m = jnp.max(s, axis=0, keepdims=True)
        p = jnp.exp2(s - m)
        acc = _dot(vt_ref[n_past], p.astype(BF16))
        yield

        if n_past > 0:
            blk_f = lax.broadcasted_iota(jnp.int32, gate.shape, 0).astype(F32)
            past = blk_f < float(n_past)
            gate = jnp.where(past, gate, NEG_INF)
            sel = jnp.zeros(gate.shape, F32)
            for _ in range(min(MOBA_TOPK, n_blk)):
                top = jnp.max(gate, axis=0, keepdims=True)
                first = jnp.min(jnp.where(gate == top, blk_f, float(n_pad)), axis=0, keepdims=True)
                pick = blk_f == first
                sel = jnp.where(jnp.logical_and(pick, past), 1.0, sel)
                gate = jnp.where(pick, -jnp.inf, gate)

            for n in range(n_past):
                chosen = sel[n:n + 1, :] > 0.5
                s = s_past[n * bs:(n + 1) * bs]
                m_blk = jnp.where(chosen, jnp.max(s, axis=0, keepdims=True), NEG_INF)
                m_new = jnp.maximum(m, m_blk)
                corr = jnp.exp2(m - m_new)
                p = jnp.exp2(s - jnp.where(chosen, m_new, jnp.inf))
                acc = acc * corr + _dot(vt_ref[n], p.astype(BF16))
                m = m_new
                yield

        o = acc[0:HEAD_DIM] / acc[HEAD_DIM:HEAD_DIM + 1]
        o_ref[n_past * bs:(n_past + 1) * bs, :] = o.T.astype(o_ref.dtype)

    def tile_group(i):
        half = per_step // 2
        per_half = n_blk // per_step
        tiles = []
        for g in range(half):
            lo = g * 2 * per_half + i
            tiles += [n_blk - 1 - lo, lo]
        progs = [tile(n) for n in sorted(tiles, reverse=True)]
        while progs:
            progs = [prog for prog in progs if next(prog, "done") != "done"]

    for i in range(n_blk // per_step):
        pl.when(step == i)(functools.partial(tile_group, i))


def _moba_attention(q, k, vt):
    batch, seq, _ = q.shape
    n_blk = seq // MOBA_BLOCK
    n_pad = -(-n_blk // SUBLANES) * SUBLANES
    per_step = min(MOBA_TILES_PER_STEP, n_blk)
    assert seq % MOBA_BLOCK == 0 and per_step % 2 == 0 and n_blk % per_step == 0
    q_spec = pl.BlockSpec((None, seq, HEAD_DIM), lambda b, h, i: (b, 0, h))
    k_spec = q_spec
    vt_spec = pl.BlockSpec((None, None, n_blk, MOBA_VT_ROWS, MOBA_BLOCK), lambda b, h, i: (b, h, 0, 0, 0))
    return pl.pallas_call(
        functools.partial(_moba_attn_kernel, n_blk=n_blk, n_pad=n_pad, per_step=per_step),
        grid=(batch, MOBA_HEADS, n_blk // per_step),
        in_specs=[q_spec, k_spec, vt_spec],
        out_specs=q_spec,
        out_shape=jax.ShapeDtypeStruct(q.shape, BF16),
        scratch_shapes=[
            pltpu.VMEM((n_pad, HEAD_DIM), F32),
        ],
        compiler_params=_params(("parallel", "parallel", "arbitrary")),
        name="moba_attn",
    )(q, k, vt)


GDN_IN_ROWS = 256


def _gdn_in_kernel(h_ref, wqkv_ref, wb_ref, wa_ref, cw_ref, alog_ref, dtb_ref,
                   q_ref, k_ref, v_ref, beta_ref, gc_ref, xs_ref):
    tm = GDN_IN_ROWS
    halo = SUBLANES
    hb = h_ref[...].astype(BF16)

    @pl.when(pl.program_id(1) == 0)
    def _():
        xs_ref[0:halo, :] = jnp.zeros((halo, GDN_CONV_DIM), F32)

    xs_ref[halo:halo + tm, :] = _dot(hb, wqkv_ref[...])

    n_cols = GDN_CONV_DIM // HEAD_DIM
    for c in range(n_cols):
        cs = slice(c * HEAD_DIM, (c + 1) * HEAD_DIM)
        xe = xs_ref[:, cs]
        y = xe[halo:] * cw_ref[GDN_CONV - 1:GDN_CONV, cs]
        for back in range(1, GDN_CONV):
            i = GDN_CONV - 1 - back
            y = y + pltpu.roll(xe, back, axis=0)[halo:] * cw_ref[i:i + 1, cs]
        y = y * _sigmoid(y)
        if c < 2 * GDN_QK_HEADS:
            y = y * lax.rsqrt(jnp.sum(y * y, axis=-1, keepdims=True) + RMS_EPS)
            if c < GDN_QK_HEADS:
                q_ref[c] = (y * (HEAD_DIM ** -0.5)).astype(BF16)
            else:
                k_ref[c - GDN_QK_HEADS] = y.astype(BF16)
        else:
            v_ref[c - 2 * GDN_QK_HEADS] = y.astype(BF16)

    xs_ref[0:halo, :] = xs_ref[tm:tm + halo, :]

    beta = _sigmoid(_dot(hb, wb_ref[...]))
    a = _dot(hb, wa_ref[...]) + dtb_ref[...]
    softplus = jnp.maximum(a, 0.0) + jnp.log1p(jnp.exp(-jnp.abs(a)))
    g = -jnp.exp(alog_ref[...]) * softplus
    ri = lax.broadcasted_iota(jnp.int32, (tm, tm), 0)
    ci = lax.broadcasted_iota(jnp.int32, (tm, tm), 1)
    tri = jnp.logical_and(ri >= ci, ri // GDN_CHUNK == ci // GDN_CHUNK).astype(F32)
    gc = jnp.dot(tri, g, preferred_element_type=F32, precision=lax.Precision.HIGHEST)
    beta_ref[...] = beta[:, :GDN_V_HEADS]
    gc_ref[...] = gc[:, :GDN_V_HEADS]


def _gdn_in(h3, wqkv, wb, wa, conv_w, alog, dtb):
    batch, seq, d = h3.shape
    tm = GDN_IN_ROWS

    def rows(width):
        return pl.BlockSpec((None, tm, width), lambda b, t: (b, t, 0))

    def heads(n):
        return pl.BlockSpec((None, n, tm, HEAD_DIM), lambda b, t: (b, 0, t, 0))

    def head_out(n):
        return jax.ShapeDtypeStruct((batch, n, seq, HEAD_DIM), BF16)

    gate_out = jax.ShapeDtypeStruct((batch, seq, GDN_V_HEADS), F32)
    return pl.pallas_call(
        _gdn_in_kernel,
        grid=(batch, seq // tm),
        in_specs=[rows(d), _resident(wqkv.shape), _resident(wb.shape), _resident(wa.shape),
                  _resident(conv_w.shape), _resident(alog.shape), _resident(dtb.shape)],
        out_specs=[heads(GDN_QK_HEADS), heads(GDN_QK_HEADS), heads(GDN_V_HEADS),
                   rows(GDN_V_HEADS), rows(GDN_V_HEADS)],
        out_shape=[head_out(GDN_QK_HEADS), head_out(GDN_QK_HEADS), head_out(GDN_V_HEADS),
                   gate_out, gate_out],
        scratch_shapes=[pltpu.VMEM((tm + SUBLANES, GDN_CONV_DIM), F32)],
        compiler_params=_params(("parallel", "arbitrary")),
        name="gdn_in",
    )(h3, wqkv, wb, wa, conv_w, alog, dtb)


GDN_TILE = 256


def _gdn_core_kernel(q_ref, k_ref, v_ref, gcol_ref, bcol_ref, grow_ref, o_ref,
                     s_ref, u_ref, wq_ref, kd_ref, attn_ref, gl_ref, bd_ref):
    tile = GDN_TILE
    c_len = GDN_CHUNK
    n_chunks = tile // c_len
    group = GDN_V_HEADS // GDN_QK_HEADS

    @pl.when(pl.program_id(1) == 0)
    def _():
        s_ref[...] = jnp.zeros_like(s_ref)

    ii = lax.broadcasted_iota(jnp.int32, (tile, tile), 0)
    jj = lax.broadcasted_iota(jnp.int32, (tile, tile), 1)
    bd_ref[...] = ((ii // c_len) == (jj // c_len)).astype(BF16)
    ci = lax.broadcasted_iota(jnp.int32, (c_len, tile), 0)
    cj = lax.broadcasted_iota(jnp.int32, (c_len, tile), 1)
    cj_in = cj % c_len
    cj_chunk = cj // c_len
    cat_lower = ci >= cj_in
    cat_strict = ci > cj_in
    eye_cat = (ci == cj_in).astype(F32)
    half_lane = lax.broadcasted_iota(jnp.int32, (c_len, LANES), 1) < c_len
    head_lane = lax.broadcasted_iota(jnp.int32, (tile, GDN_V_HEADS), 1)

    def diag_blocks(x_full):
        out = x_full[0:c_len]
        for c in range(1, n_chunks):
            out = jnp.where(cj_chunk == c, x_full[c * c_len:(c + 1) * c_len], out)
        return out

    def rows_to_cat(x_b):
        halves = []
        for c in range(0, n_chunks, 2):
            halves.append(jnp.where(half_lane, x_b[c * c_len:(c + 1) * c_len],
                                    x_b[(c + 1) * c_len:(c + 2) * c_len]))
        return jnp.concatenate(halves, axis=1)

    def join_mask(s_blk):
        return jnp.logical_and(ci // (2 * s_blk) == cj_in // (2 * s_blk),
                               jnp.logical_and((ci // s_blk) % 2 == 1, (cj_in // s_blk) % 2 == 0))

    joins = {}
    s_blk = 1
    while s_blk < c_len:
        joins[s_blk] = join_mask(s_blk)
        s_blk *= 2

    def expand(x_cat):
        return jnp.concatenate([x_cat] * n_chunks, axis=0) * bd_ref[...]

    def prepare(hk, q_t, k_t, kk_cat, qk_cat, j):
        hv = hk * group + j
        pick = head_lane == hv
        gcc = jnp.sum(jnp.where(pick, gcol_ref[...], 0.0), axis=-1, keepdims=True)
        beta = jnp.sum(jnp.where(pick, bcol_ref[...], 0.0), axis=-1, keepdims=True)
        gcc_b = jnp.broadcast_to(gcc, (tile, LANES))
        beta_b = jnp.broadcast_to(beta, (tile, LANES))
        gcr = grow_ref[hv]
        decay = jnp.exp(jnp.where(cat_lower, rows_to_cat(gcc_b) - gcr, NEG_INF))
        attn_cat = (qk_cat * decay).astype(BF16)
        for c in range(n_chunks):
            attn_ref[hv, c] = attn_cat[:, c * c_len:(c + 1) * c_len]
        p_cat = jnp.where(cat_strict, -(rows_to_cat(beta_b) * kk_cat * decay), 0.0)
        gam_b = jnp.exp(gcc_b)
        k32 = k_t.astype(F32)
        v32 = v_ref[hv].astype(F32)
        rhs = jnp.concatenate([v32 * beta_b, k32 * (beta_b * gam_b)], axis=1).astype(BF16)
        q_dec = (q_t.astype(F32) * gam_b).astype(BF16)
        g_last_b = jnp.concatenate(
            [jnp.broadcast_to(gcc_b[(c + 1) * c_len - 1:(c + 1) * c_len], (c_len, LANES))
             for c in range(n_chunks)], axis=0)
        kd_ref[hv] = (k32 * jnp.exp(g_last_b - gcc_b)).astype(BF16)
        for c in range(n_chunks):
            wq_ref[hv, (2 * c + 1) * c_len:(2 * c + 2) * c_len] = q_dec[c * c_len:(c + 1) * c_len]
            gl_ref[hv, c] = jnp.exp(g_last_b[c * c_len:c * c_len + SUBLANES])
        return dict(hv=hv, p_cat=p_cat, rhs=rhs)

    def head_program(hk, j, shared):
        if not shared:
            q_t = q_ref[hk]
            k_t = k_ref[hk]
            shared.update(q_t=q_t, k_t=k_t, kk_cat=diag_blocks(_dot_nt(k_t, k_t)),
                          qk_cat=diag_blocks(_dot_nt(q_t, k_t)))
        h = prepare(hk, shared["q_t"], shared["k_t"], shared["kk_cat"], shared["qk_cat"], j)
        yield
        p_cat = h["p_cat"]
        t_cat = eye_cat + jnp.where(joins[1], p_cat, 0.0)
        yield
        s_blk = 2
        while s_blk < c_len:
            low = jnp.where(joins[s_blk], p_cat, 0.0).astype(BF16)
            m_cat = _dot(low, expand(t_cat.astype(BF16)))
            yield
            t_cat = t_cat + _dot(t_cat.astype(BF16), expand(m_cat.astype(BF16)))
            s_blk *= 2
            yield
        uw = _dot(expand(t_cat.astype(BF16)), h["rhs"])
        hv = h["hv"]
        u_ref[hv] = uw[:, :HEAD_DIM]
        w_b = uw[:, HEAD_DIM:].astype(BF16)
        for c in range(n_chunks):
            wq_ref[hv, 2 * c * c_len:(2 * c + 1) * c_len] = w_b[c * c_len:(c + 1) * c_len]

    progs = []
    for hk in range(GDN_QK_HEADS):
        shared = {}
        for j in range(group):
            progs.append(head_program(hk, j, shared))
    while progs:
        progs = [prog for prog in progs if next(prog, "done") != "done"]

    for c in range(n_chunks):
        rows = slice(c * c_len, (c + 1) * c_len)
        boths = [_dot(wq_ref[hv, 2 * c * c_len:(2 * c + 2) * c_len, :], s_ref[hv].astype(BF16))
                 for hv in range(GDN_V_HEADS)]
        for hv in range(GDN_V_HEADS):
            v_new = (u_ref[hv, rows, :] - boths[hv][:c_len]).astype(BF16)
            o_ref[rows, hv * HEAD_DIM:(hv + 1) * HEAD_DIM] = (
                boths[hv][c_len:] + _dot(attn_ref[hv, c], v_new)).astype(o_ref.dtype)
            s_ref[hv] = s_ref[hv] * gl_ref[hv, c][0:1, :] + lax.dot_general(
                kd_ref[hv, rows, :], v_new, TN_DIMS, preferred_element_type=F32)


def _gdn_core(q, k, v, gc, beta):
    batch, _, seq, _ = q.shape
    tile = GDN_TILE
    n_chunks = tile // GDN_CHUNK
    nh = GDN_V_HEADS
    g_row = gc.transpose(0, 2, 1).reshape(batch, nh, seq // tile, 1, tile)

    def heads(n):
        return pl.BlockSpec((None, n, tile, HEAD_DIM), lambda b, t: (b, 0, t, 0))

    col_spec = pl.BlockSpec((None, tile, nh), lambda b, t: (b, t, 0))
    row_spec = pl.BlockSpec((None, nh, None, 1, tile), lambda b, t: (b, 0, t, 0, 0))
    return pl.pallas_call(
        _gdn_core_kernel,
        grid=(batch, seq // tile),
        in_specs=[heads(GDN_QK_HEADS), heads(GDN_QK_HEADS), heads(nh), col_spec, col_spec, row_spec],
        out_specs=pl.BlockSpec((None, tile, GDN_V_DIM), lambda b, t: (b, t, 0)),
        out_shape=jax.ShapeDtypeStruct((batch, seq, GDN_V_DIM), BF16),
        scratch_shapes=[
            pltpu.VMEM((nh, HEAD_DIM, HEAD_DIM), F32),
            pltpu.VMEM((nh, tile, HEAD_DIM), F32),
            pltpu.VMEM((nh, 2 * tile, HEAD_DIM), BF16),
            pltpu.VMEM((nh, tile, HEAD_DIM), BF16),
            pltpu.VMEM((nh, n_chunks, GDN_CHUNK, GDN_CHUNK), BF16),
            pltpu.VMEM((nh, n_chunks, SUBLANES, LANES), F32),
            pltpu.VMEM((tile, tile), BF16),
        ],
        compiler_params=_params(("parallel", "arbitrary")),
        name="gdn_core",
    )(q, k, v, gc, beta, g_row)


def _run_pipelined(progs):
    pending = list(progs)
    active = []
    while pending or active:
        if pending:
            active.append(pending.pop(0))
        active = [prog for prog in active if next(prog, "done") != "done"]


def _ffn_stages(h1, win_ref, wout_ref, g_ref, b_ref, o_ref, rows):
    hb = h1.astype(BF16)
    gate = _dot(hb, win_ref[:, :D_FF])
    up = _dot(hb, win_ref[:, D_FF:])
    yield
    act = (gate * _sigmoid(gate) * up).astype(BF16)
    y = ALPHA * h1 + HALF_STEP * _dot(act, wout_ref[...])
    yield
    o_ref[rows, :] = _layer_norm(y, g_ref[...], b_ref[...])


MOBA_POST_ROWS = 1024
MOBA_POST_SUBTILES = 4


def _moba_post_kernel(h_ref, x_ref, wm_ref, g1_ref, b1_ref, win_ref, wout_ref, g2_ref, b2_ref, o_ref):
    sub = MOBA_POST_ROWS // MOBA_POST_SUBTILES

    def subtile(r):
        rows = slice(r * sub, (r + 1) * sub)
        mix = _dot(x_ref[rows, :], wm_ref[...])
        yield
        h1 = _layer_norm(ALPHA * h_ref[rows, :] + mix, g1_ref[...], b1_ref[...])
        yield from _ffn_stages(h1, win_ref, wout_ref, g2_ref, b2_ref, o_ref, rows)

    _run_pipelined([subtile(r) for r in range(MOBA_POST_SUBTILES)])


def _moba_post(h, x, wm, g1, b1, w_in, w_out, g2, b2):
    m, d = h.shape
    tm = MOBA_POST_ROWS
    return pl.pallas_call(
        _moba_post_kernel,
        grid=(m // tm,),
        in_specs=[pl.BlockSpec((tm, d), lambda i: (i, 0)),
                  pl.BlockSpec((tm, x.shape[1]), lambda i: (i, 0)),
                  _resident(wm.shape), _resident(g1.shape), _resident(b1.shape),
                  _resident(w_in.shape), _resident(w_out.shape), _resident(g2.shape), _resident(b2.shape)],
        out_specs=pl.BlockSpec((tm, d), lambda i: (i, 0)),
        out_shape=jax.ShapeDtypeStruct((m, d), F32),
        compiler_params=_params(("parallel",)),
        name="moba_post_ffn",
    )(h, x, wm, g1, b1, w_in, w_out, g2, b2)


GDN_POST_ROWS = 512
GDN_POST_SUBTILES = 2


def _gdn_post_kernel(h_ref, o_ref, wz_ref, nw_ref, wo_ref, g1_ref, b1_ref,
                     win_ref, wout_ref, g2_ref, b2_ref, out_ref):
    sub = GDN_POST_ROWS // GDN_POST_SUBTILES

    def subtile(r):
        rows = slice(r * sub, (r + 1) * sub)
        h = h_ref[rows, :]
        z = _dot(h.astype(BF16), wz_ref[...])
        yield
        nw = nw_ref[...]
        gated = []
        for n in range(GDN_V_HEADS):
            cs = slice(n * HEAD_DIM, (n + 1) * HEAD_DIM)
            o = o_ref[rows, cs].astype(F32)
            zz = z[:, cs]
            y = o * lax.rsqrt(jnp.mean(o * o, axis=-1, keepdims=True) + RMS_EPS) * nw * (zz * _sigmoid(zz))
            gated.append(y.astype(BF16))
        mix = _dot(jnp.concatenate(gated, axis=1), wo_ref[...])
        yield
        h1 = _layer_norm(ALPHA * h + mix, g1_ref[...], b1_ref[...])
        yield from _ffn_stages(h1, win_ref, wout_ref, g2_ref, b2_ref, out_ref, rows)

    _run_pipelined([subtile(r) for r in range(GDN_POST_SUBTILES)])


def _gdn_post(h, o, wz, nw, wo, g1, b1, w_in, w_out, g2, b2):
    m, d = h.shape
    tm = GDN_POST_ROWS
    return pl.pallas_call(
        _gdn_post_kernel,
        grid=(m // tm,),
        in_specs=[pl.BlockSpec((tm, d), lambda i: (i, 0)),
                  pl.BlockSpec((tm, GDN_V_DIM), lambda i: (i, 0)),
                  _resident(wz.shape), _resident(nw.shape), _resident(wo.shape),
                  _resident(g1.shape), _resident(b1.shape),
                  _resident(w_in.shape), _resident(w_out.shape), _resident(g2.shape), _resident(b2.shape)],
        out_specs=pl.BlockSpec((tm, d), lambda i: (i, 0)),
        out_shape=jax.ShapeDtypeStruct((m, d), F32),
        compiler_params=_params(("parallel",)),
        name="gdn_post_ffn",
    )(h, o, wz, nw, wo, g1, b1, w_in, w_out, g2, b2)


def _row(vec):
    return vec.reshape(1, -1).astype(F32)


def _pad_lanes(x2d):
    return jnp.pad(x2d, ((0, 0), (0, LANES - x2d.shape[1])))


def kernel(x, ln_g, ln_b, ffn_pre_w_in, ffn_pre_w_out, ffn_post_w_in, ffn_post_w_out,
           moba_w_in, moba_w_out, gdn_w_in, gdn_conv_w, gdn_a_log, gdn_dt_bias,
           gdn_norm_w, gdn_w_out):
    batch, seq, d = x.shape
    m = batch * seq
    h = x.reshape(m, d).astype(F32)
    cos, sin = _rope_tables(seq)

    for i in range(DEPTH):
        h = _ffn_block(h, ffn_pre_w_in[i].astype(BF16), ffn_pre_w_out[i].astype(BF16),
                       _row(ln_g[i, 0]), _row(ln_b[i, 0]))
        post = (ffn_post_w_in[i].astype(BF16), ffn_post_w_out[i].astype(BF16),
                _row(ln_g[i, 2]), _row(ln_b[i, 2]))
        j = i // 2
        if i % 2 == 0:
            w_in = moba_w_in[j]
            q, k, vt = _qkv_rope(h.reshape(batch, seq, d), w_in[:, :2 * D_MODEL].astype(BF16),
                                 w_in[:, 2 * D_MODEL:].T.astype(BF16), cos, sin)
            o = _moba_attention(q, k, vt)
            h = _moba_post(h, o.reshape(m, d), moba_w_out[j].astype(BF16),
                           _row(ln_g[i, 1]), _row(ln_b[i, 1]), *post)
        else:
            w_in = gdn_w_in[j]
            z0 = GDN_CONV_DIM
            b0 = z0 + GDN_V_DIM
            a0 = b0 + GDN_V_HEADS
            wqkv = w_in[:, :z0].astype(BF16)
            wz = w_in[:, z0:b0].astype(BF16)
            wb = _pad_lanes(w_in[:, b0:a0]).astype(BF16)
            wa = _pad_lanes(w_in[:, a0:a0 + GDN_V_HEADS]).astype(BF16)
            q, k, v, beta, gc = _gdn_in(
                h.reshape(batch, seq, d), wqkv, wb, wa, gdn_conv_w[j].astype(F32),
                _pad_lanes(_row(gdn_a_log[j])), _pad_lanes(_row(gdn_dt_bias[j])))
            o = _gdn_core(q, k, v, gc, beta)
            h = _gdn_post(h, o.reshape(m, GDN_V_DIM), wz, _row(gdn_norm_w[j]),
                          gdn_w_out[j].astype(BF16), _row(ln_g[i, 1]), _row(ln_b[i, 1]), *post)
    return h.reshape(batch, seq, d).astype(x.dtype)
```
